```python
import math
import jax, jax.numpy as jnp
from jax import lax
import numpy as np

D_MODEL = 1024
BATCH = 2
SEQ = 8192
DEPTH = 2
DEC_BATCH = 32
DEC_SEQ = 4
PAST_LEN = 16384
PAGE_SIZE = 128

CONV_GROUPS = 4
CONV_W = D_MODEL // 4
CONV_K = 3
DA_HEADS = 4
DA_QK = D_MODEL // 16
DA_V = 2 * DA_QK
FOX_HEADS = 4
FOX_HD = D_MODEL // 16
MIX_W = CONV_W + DA_HEADS * DA_V + FOX_HEADS * FOX_HD
D_FF = ((8 * D_MODEL // 3 + 127) // 128) * 128
FFN_K = 3
NUM_BUCKETS = 32
MAX_DISTANCE = 128
Q_BLOCK = 128
RMS_EPS = 1e-6
NEG_INF = -1e30
FORGET_BIAS_INIT = 3.0
DA_SCALE = DA_QK ** -0.5
FOX_SCALE = FOX_HD ** -0.5
IN_SPLITS = (CONV_W, CONV_W, CONV_W, 2 * DA_HEADS * DA_QK, 2 * DA_HEADS * DA_QK, DA_HEADS * DA_V, FOX_HEADS * FOX_HD, FOX_HEADS * FOX_HD, FOX_HEADS * FOX_HD, FOX_HEADS)
IN_W = sum(IN_SPLITS)

kernel_name = 'hybrid_conv_diff_fox_decode_step'


def _rmsnorm(x, g):
    xf = x.astype(jnp.float32)
    y = xf * lax.rsqrt(jnp.mean(xf * xf, axis=-1, keepdims=True) + RMS_EPS)
    return (y * g.astype(jnp.float32)).astype(x.dtype)


def _t5_bucket(rel):
    n = jnp.maximum(rel, 0)
    max_exact = NUM_BUCKETS // 2
    nf = jnp.maximum(n, 1).astype(jnp.float32)
    large = max_exact + (jnp.log(nf / max_exact) / math.log(MAX_DISTANCE / max_exact) * (NUM_BUCKETS - max_exact)).astype(jnp.int32)
    return jnp.where(n < max_exact, n, jnp.minimum(large, NUM_BUCKETS - 1))


def _rel_bias(table, q_pos, k_pos):
    b = _t5_bucket(q_pos[:, None] - k_pos[None, :])
    return jnp.moveaxis(table.astype(jnp.float32)[b], -1, 0)


def _causal_dwconv(u, prefix, w):
    k = w.shape[0]
    t = u.shape[1]
    full = jnp.concatenate([prefix.astype(u.dtype), u], axis=1)
    y = full[:, 0:t] * w[0]
    for j in range(1, k):
        y = y + full[:, j:j + t] * w[j]
    return y, full[:, t:]


def _paged(pool, layer, page_table):
    g = pool[layer, page_table]
    return g.reshape((g.shape[0], g.shape[1] * g.shape[2]) + g.shape[3:])


def _blocked(fn, q_arrays):
    b, s = q_arrays[0].shape[:2]
    nb = s // Q_BLOCK
    blocks = tuple(jnp.swapaxes(a.reshape((b, nb, Q_BLOCK) + a.shape[2:]), 0, 1) for a in q_arrays)

    def body(args):
        i, qs = args
        return fn(i * Q_BLOCK + jnp.arange(Q_BLOCK), *qs)

    out = jnp.swapaxes(lax.map(body, (jnp.arange(nb), blocks)), 0, 1)
    return out.reshape((b, s) + out.shape[3:])


def _diff_core(q, q_pos, segs, lam, table):
    s = jnp.concatenate([jnp.einsum('bqhjd,bkhjd->jbhqk', q, k, preferred_element_type=jnp.float32) for k, _, _ in segs], axis=-1) * DA_SCALE
    k_pos = jnp.concatenate([p for _, _, p in segs])
    s = s + _rel_bias(table, q_pos, k_pos)
    s = jnp.where(q_pos[:, None] >= k_pos[None, :], s, NEG_INF)
    p = jax.nn.softmax(s, axis=-1)
    a = p[0] - lam * p[1]
    out, off = None, 0
    for _, v, kp in segs:
        n = kp.shape[0]
        o = jnp.einsum('bhqk,bkhd->bqhd', a[..., off:off + n], v)
        out = o if out is None else out + o
        off += n
    return out.astype(q.dtype)


def _fox_core(q, c_q, q_pos, segs):
    s = jnp.concatenate([jnp.einsum('bqhd,bkhd->bhqk', q, k, preferred_element_type=jnp.float32) for k, _, _, _ in segs], axis=-1) * FOX_SCALE
    c_k = jnp.concatenate([c for _, _, c, _ in segs], axis=1).astype(jnp.float32)
    k_pos = jnp.concatenate([p for _, _, _, p in segs])
    s = s + jnp.swapaxes(c_q.astype(jnp.float32), 1, 2)[:, :, :, None] - jnp.swapaxes(c_k, 1, 2)[:, :, None, :]
    s = jnp.where(q_pos[:, None] >= k_pos[None, :], s, NEG_INF)
    p = jax.nn.softmax(s, axis=-1)
    out, off = None, 0
    for _, v, _, kp in segs:
        n = kp.shape[0]
        o = jnp.einsum('bhqk,bkhd->bqhd', p[..., off:off + n], v)
        out = o if out is None else out + o
        off += n
    return out.astype(q.dtype)


def _layer(x, lam_init, conv_prefix, ffn_prefix, diff_attend, fox_attend, w_in_l, b_f_l, conv_w_l, lam_l, subln_l, w_out_l, norm1_l, norm2_l, w_up_l, ffn_conv_l, w_down_l):
    b, t, _ = x.shape
    h = _rmsnorm(x, norm1_l)
    proj = h @ w_in_l
    idx = [int(i) for i in np.cumsum(IN_SPLITS)[:-1]]
    u, gb, gc, q, k, v, qf, kf, vf, fl = jnp.split(proj, idx, axis=-1)
    cy, conv_state = _causal_dwconv(gc * u, conv_prefix, conv_w_l)
    conv_out = gb * cy
    q = q.reshape(b, t, DA_HEADS, 2, DA_QK)
    k = k.reshape(b, t, DA_HEADS, 2, DA_QK)
    v = v.reshape(b, t, DA_HEADS, DA_V)
    lp = lam_l.astype(jnp.float32)
    lam = jnp.exp(jnp.sum(lp[0] * lp[1])) - jnp.exp(jnp.sum(lp[2] * lp[3])) + lam_init
    od = _rmsnorm(diff_attend(q, k, v, lam), subln_l) * (1.0 - lam_init)
    qf = qf.reshape(b, t, FOX_HEADS, FOX_HD)
    kf = kf.reshape(b, t, FOX_HEADS, FOX_HD)
    vf = vf.reshape(b, t, FOX_HEADS, FOX_HD)
    logf = jax.nn.log_sigmoid((fl + b_f_l).astype(jnp.float32))
    of = fox_attend(qf, kf, vf, logf)
    mix = jnp.concatenate([conv_out, od.reshape(b, t, -1), of.reshape(b, t, -1)], axis=-1)
    x = x + mix @ w_out_l
    up = _rmsnorm(x, norm2_l) @ w_up_l
    upc, ffn_state = _causal_dwconv(up, ffn_prefix, ffn_conv_l)
    val, gate = jnp.split(upc, 2, axis=-1)
    x = x + (jax.nn.silu(gate) * val) @ w_down_l
    return x, (k, v, kf, vf, logf, conv_state, ffn_state)


def setup_inputs(seed: int = 0) -> dict:
    key = jax.random.key(seed)
    ks = jax.random.split(key, 24)
    f32 = jnp.float32

    def nrm(k, shape, scale=1.0):
        return scale * jax.random.normal(k, shape, f32)

    n_pages = PAST_LEN // PAGE_SIZE
    n_used = DEC_BATCH * n_pages
    n_phys = n_used + n_used // 4
    page_table = jax.random.permutation(ks[9], n_phys)[:n_used].reshape(DEC_BATCH, n_pages).astype(jnp.int32)
    return {
        'x_prompt': nrm(ks[0], (BATCH, SEQ, D_MODEL)),
        'x_sample': nrm(ks[1], (DEC_BATCH, DEC_SEQ, D_MODEL)),
        'cache_dk': nrm(ks[2], (DEPTH, n_phys, PAGE_SIZE, DA_HEADS, 2, DA_QK)),
        'cache_dv': nrm(ks[3], (DEPTH, n_phys, PAGE_SIZE, DA_HEADS, DA_V)),
        'cache_fk': nrm(ks[4], (DEPTH, n_phys, PAGE_SIZE, FOX_HEADS, FOX_HD)),
        'cache_fv': nrm(ks[5], (DEPTH, n_phys, PAGE_SIZE, FOX_HEADS, FOX_HD)),
        'cache_flogf': jax.nn.log_sigmoid(FORGET_BIAS_INIT + nrm(ks[6], (DEPTH, n_phys, PAGE_SIZE, FOX_HEADS))),
        'state_conv': nrm(ks[7], (DEPTH, DEC_BATCH, CONV_K - 1, CONV_W)),
        'state_ffn': nrm(ks[8], (DEPTH, DEC_BATCH, FFN_K - 1, 2 * D_FF)),
        'page_table': page_table,
        'rel_bias': nrm(ks[10], (NUM_BUCKETS, DA_HEADS), 0.5),
        'norm1': 1.0 + nrm(ks[11], (DEPTH, D_MODEL), 0.1),
        'w_in': nrm(ks[12], (DEPTH, D_MODEL, IN_W), D_MODEL ** -0.5),
        'b_f': FORGET_BIAS_INIT + nrm(ks[13], (DEPTH, FOX_HEADS), 0.5),
        'conv_w': nrm(ks[14], (DEPTH, CONV_K, CONV_W), CONV_K ** -0.5),
        'diff_lambda': nrm(ks[15], (DEPTH, 4, DA_QK), 0.1),
        'subln': 1.0 + nrm(ks[16], (DEPTH, DA_V), 0.1),
        'w_out': nrm(ks[17], (DEPTH, MIX_W, D_MODEL), MIX_W ** -0.5),
        'norm2': 1.0 + nrm(ks[18], (DEPTH, D_MODEL), 0.1),
        'w_up': nrm(ks[19], (DEPTH, D_MODEL, 2 * D_FF), D_MODEL ** -0.5),
        'ffn_conv': nrm(ks[20], (DEPTH, FFN_K, 2 * D_FF), FFN_K ** -0.5),
        'w_down': nrm(ks[21], (DEPTH, D_FF, D_MODEL), D_FF ** -0.5),
        'norm_f': 1.0 + nrm(ks[22], (D_MODEL,), 0.1),
    }


def reference(x_prompt, x_sample, cache_dk, cache_dv, cache_fk, cache_fv, cache_flogf, state_conv, state_ffn, page_table, rel_bias, norm1, w_in, b_f, conv_w, diff_lambda, subln, w_out, norm2, w_up, ffn_conv, w_down, norm_f):
    xp, xs = x_prompt, x_sample
    bp, s_len = xp.shape[:2]
    past = page_table.shape[1] * PAGE_SIZE
    kpos_p = jnp.arange(s_len)
    kpos_past = jnp.arange(past)
    st_p, st_s = [], []
    for l in range(DEPTH):
        lam_init = 0.8 - 0.6 * math.exp(-0.3 * l)
        lw = (w_in[l], b_f[l], conv_w[l], diff_lambda[l], subln[l], w_out[l], norm1[l], norm2[l], w_up[l], ffn_conv[l], w_down[l])

        def diff_p(q, k, v, lam):
            return _blocked(lambda qp, qb: _diff_core(qb, qp, [(k, v, kpos_p)], lam, rel_bias), (q,))

        def fox_p(q, k, v, logf):
            c = jnp.cumsum(logf, axis=1)
            return _blocked(lambda qp, qb, cb: _fox_core(qb, cb, qp, [(k, v, c, kpos_p)]), (q, c))

        zc = jnp.zeros((bp, CONV_K - 1, CONV_W), xp.dtype)
        zf = jnp.zeros((bp, FFN_K - 1, 2 * D_FF), xp.dtype)
        xp, sp = _layer(xp, lam_init, zc, zf, diff_p, fox_p, *lw)
        st_p.append(sp)

        dkp = _paged(cache_dk, l, page_table)
        dvp = _paged(cache_dv, l, page_table)
        fkp = _paged(cache_fk, l, page_table)
        fvp = _paged(cache_fv, l, page_table)
        flp = _paged(cache_flogf, l, page_table)

        def diff_s(q, k, v, lam):
            qpos = past + jnp.arange(q.shape[1])
            return _diff_core(q, qpos, [(dkp, dvp, kpos_past), (k, v, qpos)], lam, rel_bias)

        def fox_s(q, k, v, logf):
            qpos = past + jnp.arange(q.shape[1])
            c = jnp.cumsum(jnp.concatenate([flp.astype(jnp.float32), logf], axis=1), axis=1)
            c_new = c[:, past:]
            return _fox_core(q, c_new, qpos, [(fkp, fvp, c[:, :past], kpos_past), (k, v, c_new, qpos)])

        xs, ss = _layer(xs, lam_init, state_conv[l], state_ffn[l], diff_s, fox_s, *lw)
        st_s.append(ss)

    y_prompt = _rmsnorm(xp, norm_f)
    y_sample = _rmsnorm(xs, norm_f)
    dk_p = jnp.stack([s[0] for s in st_p])
    dk_s = jnp.stack([s[0] for s in st_s])
    dv_p = jnp.stack([s[1] for s in st_p])
    dv_s = jnp.stack([s[1] for s in st_s])
    fk_p = jnp.stack([s[2] for s in st_p])
    fk_s = jnp.stack([s[2] for s in st_s])
    fv_p = jnp.stack([s[3] for s in st_p])
    fv_s = jnp.stack([s[3] for s in st_s])
    fl_p = jnp.stack([s[4] for s in st_p])
    fl_s = jnp.stack([s[4] for s in st_s])
    cv_p = jnp.stack([s[5] for s in st_p])
    cv_s = jnp.stack([s[5] for s in st_s])
    ff_p = jnp.stack([s[6] for s in st_p])
    ff_s = jnp.stack([s[6] for s in st_s])
    return (y_prompt, y_sample, dk_p, dk_s, dv_p, dv_s, fk_p, fk_s, fv_p, fv_s, fl_p, fl_s, cv_p, cv_s, ff_p, ff_s)
```

```python
import functools
import math

import jax
import jax.numpy as jnp
from jax import lax
from jax.experimental import pallas as pl
from jax.experimental.pallas import tpu as pltpu

F32, BF16, I32 = jnp.float32, jnp.bfloat16, jnp.int32
RMS_EPS = 1e-6
NEG_INF = -1e30
NUM_BUCKETS = 32
MAX_DISTANCE = 128
LANES = 128
SUBLANES = 8
VMEM_LIMIT_BYTES = 56 * 2**20
NT_DIMS = (((1,), (1,)), ((), ()))


def _cparams(n_axes):
    return pltpu.CompilerParams(dimension_semantics=("arbitrary",) * n_axes, vmem_limit_bytes=VMEM_LIMIT_BYTES)


def _dot(a, b):
    return jnp.dot(a, b, preferred_element_type=F32)


def _dot_nt(a, b):
    return lax.dot_general(a, b, NT_DIMS, preferred_element_type=F32)


def _rms(x, g):
    return x * lax.rsqrt(jnp.mean(x * x, axis=-1, keepdims=True) + RMS_EPS) * g


def _log_sigmoid(x):
    return jnp.minimum(x, 0.0) - jnp.log1p(jnp.exp(-jnp.abs(x)))


def _conv3(u, p0, p1, w_ref, cols, period):
    rows = u.shape[0]
    t = lax.broadcasted_iota(I32, (rows, 1), 0)
    if period < rows:
        t = jnp.bitwise_and(t, period - 1)
    sh1 = jnp.where(t == 0, p1, pltpu.roll(u, 1, 0))
    sh2 = jnp.where(t == 0, p0, jnp.where(t == 1, p1, pltpu.roll(u, 2, 0)))
    return sh2 * w_ref[0:1, cols] + sh1 * w_ref[1:2, cols] + u * w_ref[2:3, cols]


def _prefix_sum_lanes(y, lane):
    for s in (1, 2, 4, 8, 16, 32, 64):
        y = y + jnp.where(lane >= s, pltpu.roll(y, s, 1), 0.0)
    return y


def _inproj_kernel(*refs, tm, period, prompt, da_scale, fox_scale, n_da, n_pairs):
    if prompt:
        (x_ref, g_ref, wn_ref, wt_ref, wfl_ref, bf_ref, cw_ref, pref_ref,
         conv_ref, st_ref, qz_ref, v_ref, vb_ref, qfz_ref, kt_ref, ktb_ref, kft_ref, kftb_ref,
         vft_ref, vftb_ref, lft_ref, ct_ref, ccol_ref, carry_ref, ccarry_ref) = refs
    else:
        (x_ref, g_ref, wn_ref, wt_ref, wfl_ref, bf_ref, cw_ref, p0_ref, p1_ref,
         conv_ref, st_ref, qz_ref, v_ref, vb_ref, qfz_ref, kt_ref, ktb_ref, kft_ref, kftb_ref,
         vft_ref, vftb_ref, lft_ref) = refs
    i = pl.program_id(1)
    cw = cw_ref.shape[1]
    dq = n_da * LANES
    dfx = n_pairs * LANES
    h = _rms(x_ref[0], g_ref[...]).astype(BF16)

    ugc = _dot(h, wn_ref[:, 0:3 * cw])
    gcu = ugc[:, 2 * cw:3 * cw] * ugc[:, 0:cw]
    if prompt:
        @pl.when(i == 0)
        def _():
            carry_ref[...] = pref_ref[0]
            ccarry_ref[...] = jnp.zeros_like(ccarry_ref)
        p0, p1 = carry_ref[6:7, :], carry_ref[7:8, :]
    else:
        p0, p1 = p0_ref[0], p1_ref[0]
    cy = _conv3(gcu, p0, p1, cw_ref, slice(None), period)
    conv_ref[0] = (ugc[:, cw:2 * cw] * cy).astype(BF16)
    if prompt:
        carry_ref[...] = gcu[tm - SUBLANES:tm]
        st_ref[0] = gcu[tm - SUBLANES:tm]
    else:
        st_ref[0] = gcu

    lo = lax.broadcasted_iota(I32, (tm, LANES), 1) < (LANES // 2)
    c0 = 3 * cw
    q = _dot(h, wn_ref[:, c0:c0 + dq]) * da_scale
    for hh in range(n_da):
        qh = q[:, hh * LANES:(hh + 1) * LANES]
        qz_ref[0, :, 2 * hh * LANES:(2 * hh + 1) * LANES] = jnp.where(lo, qh, 0.0).astype(BF16)
        qz_ref[0, :, (2 * hh + 1) * LANES:(2 * hh + 2) * LANES] = jnp.where(lo, 0.0, qh).astype(BF16)
    c0 += dq
    v = _dot(h, wn_ref[:, c0:c0 + dq])
    v_ref[0] = v
    vb_ref[0] = v.astype(BF16)
    c0 += dq
    qf = _dot(h, wn_ref[:, c0:c0 + dfx]) * fox_scale
    for p in range(n_pairs):
        qp = qf[:, p * LANES:(p + 1) * LANES]
        qfz_ref[0, :, 2 * p * LANES:(2 * p + 1) * LANES] = jnp.where(lo, qp, 0.0).astype(BF16)
        qfz_ref[0, :, (2 * p + 1) * LANES:(2 * p + 2) * LANES] = jnp.where(lo, 0.0, qp).astype(BF16)

    kt = _dot_nt(wt_ref[0:dq, :], h)
    kt_ref[0] = kt
    kft = _dot_nt(wt_ref[dq:dq + dfx, :], h)
    kft_ref[0] = kft
    vft = _dot_nt(wt_ref[dq + dfx:dq + 2 * dfx, :], h)
    vft_ref[0] = vft
    if prompt:
        ktb_ref[0, 0] = kt.astype(BF16)
        kftb_ref[0, 0] = kft.astype(BF16)
        vftb_ref[0, 0] = vft.astype(BF16)
    else:
        ktb_ref[0] = kt.astype(BF16)
        kftb_ref[0] = kft.astype(BF16)
        vftb_ref[0] = vft.astype(BF16)
    lf = _log_sigmoid(_dot_nt(wfl_ref[...], h)[0:SUBLANES] + bf_ref[...])
    lft_ref[0] = lf

    if prompt:
        lane = lax.broadcasted_iota(I32, (SUBLANES, LANES), 1)
        carry = ccarry_ref[...]
        blocks = []
        for blk in range(tm // LANES):
            y = _prefix_sum_lanes(lf[:, blk * LANES:(blk + 1) * LANES], lane) + carry
            carry = jnp.broadcast_to(y[:, LANES - 1:LANES], (SUBLANES, LANES))
            blocks.append(y)
        ccarry_ref[...] = carry
        ct = jnp.concatenate(blocks, axis=1)
        ct_ref[0, 0] = ct
        ccol_ref[0] = jnp.concatenate([ct, jnp.zeros((LANES - SUBLANES, tm), F32)], axis=0).T


def _inproj(x, g1, wn, wt, wfl, bft, cwp, pre, *, tm, period, prompt, da_scale, fox_scale):
    G, R, D = x.shape
    nt = R // tm
    cw = cwp.shape[1]
    dq = (wn.shape[1] - 3 * cw) * 2 // 5
    dfx = dq // 2
    n_da, n_pairs = dq // LANES, dfx // LANES
    row = lambda c: pl.BlockSpec((1, tm, c), lambda g, i: (g, i, 0))
    colT = lambda r: pl.BlockSpec((1, r, tm), lambda g, i: (g, 0, i))
    chunkT = lambda r: pl.BlockSpec((1, 1, r, tm), lambda g, i: (g, i, 0, 0))
    full = lambda a: pl.BlockSpec(a.shape, lambda g, i: (0,) * a.ndim)
    in_specs = [row(D), full(g1), full(wn), full(wt), full(wfl), full(bft), full(cwp)]
    if prompt:
        in_specs += [pl.BlockSpec((1, SUBLANES, cw), lambda g, i: (g, 0, 0))]
        ins = (x, g1, wn, wt, wfl, bft, cwp, pre)
        st_spec, st_shape = pl.BlockSpec((1, SUBLANES, cw), lambda g, i: (g, 0, 0)), (G, SUBLANES, cw)
        tb = lambda r: (chunkT(r), jax.ShapeDtypeStruct((G, nt, r, tm), BF16))
    else:
        in_specs += [row(cw), row(cw)]
        ins = (x, g1, wn, wt, wfl, bft, cwp, pre[0], pre[1])
        st_spec, st_shape = row(cw), (G, R, cw)
        tb = lambda r: (colT(r), jax.ShapeDtypeStruct((G, r, R), BF16))
    outs = [
        (row(cw), jax.ShapeDtypeStruct((G, R, cw), BF16)),
        (st_spec, jax.ShapeDtypeStruct(st_shape, F32)),
        (row(2 * dq), jax.ShapeDtypeStruct((G, R, 2 * dq), BF16)),
        (row(dq), jax.ShapeDtypeStruct((G, R, dq), F32)),
        (row(dq), jax.ShapeDtypeStruct((G, R, dq), BF16)),
        (row(2 * dfx), jax.ShapeDtypeStruct((G, R, 2 * dfx), BF16)),
        (colT(dq), jax.ShapeDtypeStruct((G, dq, R), F32)),
        tb(dq),
        (colT(dfx), jax.ShapeDtypeStruct((G, dfx, R), F32)),
        tb(dfx),
        (colT(dfx), jax.ShapeDtypeStruct((G, dfx, R), F32)),
        tb(dfx),
        (colT(SUBLANES), jax.ShapeDtypeStruct((G, SUBLANES, R), F32)),
    ]
    scratch = []
    if prompt:
        outs += [
            (chunkT(SUBLANES), jax.ShapeDtypeStruct((G, nt, SUBLANES, tm), F32)),
            (row(LANES), jax.ShapeDtypeStruct((G, R, LANES), F32)),
        ]
        scratch = [pltpu.VMEM((SUBLANES, cw), F32), pltpu.VMEM((SUBLANES, LANES), F32)]
    kern = functools.partial(_inproj_kernel, tm=tm, period=period, prompt=prompt, da_scale=da_scale,
                             fox_scale=fox_scale, n_da=n_da, n_pairs=n_pairs)
    return pl.pallas_call(
        kern, grid=(G, nt), in_specs=in_specs, out_specs=[o[0] for o in outs],
        out_shape=[o[1] for o in outs], scratch_shapes=scratch, compiler_params=_cparams(2),
        name="inproj_prompt" if prompt else "inproj_sample")(*ins)


def _bias_of_distance(n, tab_ref, h, n_heads):
    max_exact = NUM_BUCKETS // 2
    nf = jnp.maximum(n, 1).astype(F32)
    large = max_exact + (jnp.log(nf / max_exact) / math.log(MAX_DISTANCE / max_exact)
                         * (NUM_BUCKETS - max_exact)).astype(I32)
    bucket = jnp.where(n < max_exact, n, jnp.minimum(large, NUM_BUCKETS - 1))
    far = tab_ref[(NUM_BUCKETS - 1) * n_heads + h]
    out = jnp.zeros(n.shape, F32)
    for b in range(NUM_BUCKETS - 1):
        out = jnp.where(bucket == b, tab_ref[b * n_heads + h] - far, out)
    return out


def _bias_kernel(tab_ref, d_ref, bp_ref, bn_ref, *, t, past, n_heads, new_pad, dec_seq):
    r = lax.broadcasted_iota(I32, (t, t), 0)
    c = lax.broadcasted_iota(I32, (t, t), 1)
    rp = lax.broadcasted_iota(I32, (SUBLANES, past), 0)
    kp = lax.broadcasted_iota(I32, (SUBLANES, past), 1)
    rn = lax.broadcasted_iota(I32, (SUBLANES, new_pad), 0)
    kn = lax.broadcasted_iota(I32, (SUBLANES, new_pad), 1)
    tp = jnp.bitwise_and(rp, dec_seq - 1)
    tn = jnp.bitwise_and(rn, dec_seq - 1)
    for h in range(n_heads):
        d_ref[h, 0] = jnp.where(r >= c, _bias_of_distance(jnp.maximum(r - c, 0), tab_ref, h, n_heads), NEG_INF)
        d_ref[h, 1] = _bias_of_distance(t + r - c, tab_ref, h, n_heads)
        bp_ref[h * SUBLANES:(h + 1) * SUBLANES, :] = _bias_of_distance(past + tp - kp, tab_ref, h, n_heads)
        bn_ref[h * SUBLANES:(h + 1) * SUBLANES, :] = jnp.where(
            kn <= tn, _bias_of_distance(jnp.maximum(tn - kn, 0), tab_ref, h, n_heads), NEG_INF)


def _bias_tiles(rel_bias, *, t, past, new_pad, dec_seq):
    n_heads = rel_bias.shape[1]
    kern = functools.partial(_bias_kernel, t=t, past=past, n_heads=n_heads, new_pad=new_pad, dec_seq=dec_seq)
    return pl.pallas_call(
        kern,
        in_specs=[pl.BlockSpec(memory_space=pltpu.SMEM)],
        out_specs=[pl.BlockSpec(memory_space=pltpu.VMEM)] * 3,
        out_shape=[jax.ShapeDtypeStruct((n_heads, 2, t, t), F32),
                   jax.ShapeDtypeStruct((n_heads * SUBLANES, past), F32),
                   jax.ShapeDtypeStruct((n_heads * SUBLANES, new_pad), F32)],
        compiler_params=pltpu.CompilerParams(vmem_limit_bytes=VMEM_LIMIT_BYTES),
        name="rel_bias_tiles")(rel_bias.reshape(-1))


def _lambda_of(lp, lam_init):
    a = jnp.sum(lp[0:1] * lp[1:2], axis=-1, keepdims=True)
    b = jnp.sum(lp[2:3] * lp[3:4], axis=-1, keepdims=True)
    return jnp.exp(a) - jnp.exp(b) + lam_init


def _softmax_step(s, m_ref, l_ref, acc_ref, pv):
    m_old = m_ref[...]
    m_new = jnp.maximum(m_old, jnp.max(s, axis=-1, keepdims=True))
    alpha = jnp.exp(m_old - m_new)
    p = jnp.exp(s - m_new)
    l_ref[...] = alpha * l_ref[...] + jnp.sum(p, axis=-1, keepdims=True)
    acc_ref[...] = alpha * acc_ref[...] + pv(p.astype(BF16))
    m_ref[...] = m_new


def _diff_prompt_kernel(q_ref, k_ref, v_ref, d_ref, lam_ref, g_ref, o_ref,
                        m1, l1, a1, m2, l2, a2, *, t, lam_init):
    i = pl.program_id(2)
    for m, l, a in ((m1, l1, a1), (m2, l2, a2)):
        m[...] = jnp.full_like(m, NEG_INF)
        l[...] = jnp.zeros_like(l)
        a[...] = jnp.zeros_like(a)
    q1, q2 = q_ref[0, :, 0:LANES], q_ref[0, :, LANES:2 * LANES]

    def chunk(j, bias):
        kt = k_ref[0, j]
        v = v_ref[0, pl.ds(pl.multiple_of(j * t, t), t), :]
        for q, m, l, a in ((q1, m1, l1, a1), (q2, m2, l2, a2)):
            s = _dot(q, kt)
            if bias is not None:
                s = s + bias
            _softmax_step(s, m, l, a, lambda p: _dot(p, v))

    def far(j, carry):
        chunk(j, None)
        return carry

    lax.fori_loop(0, jnp.maximum(i - 1, 0), far, 0)

    @pl.when(i >= 1)
    def _():
        chunk(i - 1, d_ref[0, 1])

    chunk(i, d_ref[0, 0])
    lam = _lambda_of(lam_ref[...], lam_init)
    o = a1[...] / l1[...] - lam * (a2[...] / l2[...])
    o_ref[0] = (_rms(o, g_ref[...]) * (1.0 - lam_init)).astype(BF16)


def _diff_prompt(qz, ktb, vb, dtiles, lam_p, subln, *, t, lam_init):
    B, S, _ = qz.shape
    H = dtiles.shape[0]
    nq = S // t
    kern = functools.partial(_diff_prompt_kernel, t=t, lam_init=lam_init)
    return pl.pallas_call(
        kern, grid=(B, H, nq),
        in_specs=[pl.BlockSpec((1, t, 2 * LANES), lambda b, h, i: (b, i, h)),
                  pl.BlockSpec((1, nq, LANES, t), lambda b, h, i: (b, 0, h, 0)),
                  pl.BlockSpec((1, S, LANES), lambda b, h, i: (b, 0, h)),
                  pl.BlockSpec((1, 2, t, t), lambda b, h, i: (h, 0, 0, 0)),
                  pl.BlockSpec(lam_p.shape, lambda b, h, i: (0, 0)),
                  pl.BlockSpec(subln.shape, lambda b, h, i: (0, 0))],
        out_specs=pl.BlockSpec((1, t, LANES), lambda b, h, i: (b, i, h)),
        out_shape=jax.ShapeDtypeStruct((B, S, H * LANES), BF16),
        scratch_shapes=[pltpu.VMEM((t, 1), F32), pltpu.VMEM((t, 1), F32), pltpu.VMEM((t, LANES), F32)] * 2,
        compiler_params=_cparams(3), name="diff_attn_prompt")(qz, ktb, vb, dtiles, lam_p, subln)


def _fox_prompt_kernel(q_ref, k_ref, v_ref, c_ref, cc_ref, o_ref, m0, l0, a0, m1, l1, a1, *, t):
    pr = pl.program_id(1)
    i = pl.program_id(2)
    state = ((m0, l0, a0), (m1, l1, a1))
    for m, l, a in state:
        m[...] = jnp.full_like(m, NEG_INF)
        l[...] = jnp.zeros_like(l)
        a[...] = jnp.zeros_like(a)
    lane = lax.broadcasted_iota(I32, (t, LANES), 1)
    ccol = cc_ref[0]
    heads = []
    for e in range(2):
        hsel = lane == (2 * pr + e)
        cq = jnp.sum(jnp.where(hsel, ccol, 0.0), axis=-1, keepdims=True)
        cbase = cq[0:1, :]
        heads.append((q_ref[0, :, e * LANES:(e + 1) * LANES], cq - cbase, cbase, state[e]))

    def chunk(j, diag):
        kt = k_ref[0, j]
        vt = v_ref[0, j]
        crow = c_ref[0, j]
        crow_idx = lax.broadcasted_iota(I32, crow.shape, 0)
        for e, (q, cqr, cbase, (m, l, a)) in enumerate(heads):
            ck = jnp.sum(jnp.where(crow_idx == 2 * pr + e, crow, 0.0), axis=0, keepdims=True)
            z = _dot(q, kt) - (ck - cbase)
            if diag:
                r = lax.broadcasted_iota(I32, (t, t), 0)
                c = lax.broadcasted_iota(I32, (t, t), 1)
                z = jnp.where(r >= c, z, NEG_INF)
            m_old = m[...]
            m_new = jnp.maximum(m_old, jnp.max(z, axis=-1, keepdims=True) + cqr)
            alpha = jnp.exp(m_old - m_new)
            p = jnp.exp(z - (m_new - cqr))
            l[...] = alpha * l[...] + jnp.sum(p, axis=-1, keepdims=True)
            a[...] = alpha * a[...] + _dot_nt(p.astype(BF16), vt)
            m[...] = m_new

    def far(j, carry):
        chunk(j, False)
        return carry

    lax.fori_loop(0, i, far, 0)
    chunk(i, True)
    o_ref[0] = jnp.where(lane < LANES // 2, a0[...] / l0[...], a1[...] / l1[...]).astype(BF16)


def _fox_prompt(qfz, kftb, vftb, ct, ccol, *, t):
    B, S, _ = qfz.shape
    nq = S // t
    n_pairs = kftb.shape[2] // LANES
    kern = functools.partial(_fox_prompt_kernel, t=t)
    return pl.pallas_call(
        kern, grid=(B, n_pairs, nq),
        in_specs=[pl.BlockSpec((1, t, 2 * LANES), lambda b, p, i: (b, i, p)),
                  pl.BlockSpec((1, nq, LANES, t), lambda b, p, i: (b, 0, p, 0)),
                  pl.BlockSpec((1, nq, LANES, t), lambda b, p, i: (b, 0, p, 0)),
                  pl.BlockSpec((1, nq, SUBLANES, t), lambda b, p, i: (b, 0, 0, 0)),
                  pl.BlockSpec((1, t, LANES), lambda b, p, i: (b, i, 0))],
        out_specs=pl.BlockSpec((1, t, LANES), lambda b, p, i: (b, i, p)),
        out_shape=jax.ShapeDtypeStruct((B, S, n_pairs * LANES), BF16),
        scratch_shapes=[pltpu.VMEM((t, 1), F32), pltpu.VMEM((t, 1), F32), pltpu.VMEM((t, LANES), F32)] * 2,
        compiler_params=_cparams(3), name="fox_attn_prompt")(qfz, kftb, vftb, ct, ccol)


def _outffn_kernel(*refs, tm, period, prompt, final, chunk_w):
    if prompt:
        (x_ref, cv_ref, od_ref, of_ref, wo_ref, g2_ref, wu_ref, fw_ref, wd_ref, gf_ref, pref_ref,
         xo_ref, st_ref, act_ref, carry_ref) = refs
    else:
        (x_ref, cv_ref, od_ref, of_ref, wo_ref, g2_ref, wu_ref, fw_ref, wd_ref, gf_ref, p0_ref, p1_ref,
         xo_ref, st_ref, act_ref) = refs
    i = pl.program_id(1)
    cw, dw = cv_ref.shape[2], od_ref.shape[2]
    dff = wd_ref.shape[0]
    x1 = (x_ref[0] + _dot(cv_ref[0], wo_ref[0:cw, :]) + _dot(od_ref[0], wo_ref[cw:cw + dw, :])
          + _dot(of_ref[0], wo_ref[cw + dw:, :]))
    xn = _rms(x1, g2_ref[...]).astype(BF16)
    if prompt:
        @pl.when(i == 0)
        def _():
            carry_ref[...] = pref_ref[0]
    for c in range(dff // chunk_w):
        halves = []
        for base in (c * chunk_w, dff + c * chunk_w):
            cols = slice(base, base + chunk_w)
            up = _dot(xn, wu_ref[:, cols])
            if prompt:
                p0, p1 = carry_ref[6:7, cols], carry_ref[7:8, cols]
            else:
                p0, p1 = p0_ref[0, :, cols], p1_ref[0, :, cols]
            halves.append(_conv3(up, p0, p1, fw_ref, cols, period))
            if prompt:
                carry_ref[:, cols] = up[tm - SUBLANES:tm]
                st_ref[0, :, cols] = up[tm - SUBLANES:tm]
            else:
                st_ref[0, :, cols] = up
        val, gate = halves
        act = gate * (1.0 / (1.0 + jnp.exp(-gate))) * val
        act_ref[:, c * chunk_w:(c + 1) * chunk_w] = act.astype(BF16)
    x2 = x1 + _dot(act_ref[...], wd_ref[...])
    xo_ref[0] = _rms(x2, gf_ref[...]) if final else x2


def _outffn(x, conv, od, of, wo, g2, wu, fw, wd, gf, pre, *, tm, period, prompt, final, chunk_w):
    G, R, D = x.shape
    nt = R // tm
    dff2 = wu.shape[1]
    row = lambda c: pl.BlockSpec((1, tm, c), lambda g, i: (g, i, 0))
    once = lambda a: pl.BlockSpec(a.shape, lambda g, i: (0,) * a.ndim, pipeline_mode=pl.Buffered(1))
    in_specs = [row(D), row(conv.shape[2]), row(od.shape[2]), row(of.shape[2]),
                once(wo), once(g2), once(wu), once(fw), once(wd), once(gf)]
    if prompt:
        in_specs += [pl.BlockSpec((1, SUBLANES, dff2), lambda g, i: (g, 0, 0))]
        ins = (x, conv, od, of, wo, g2, wu, fw, wd, gf, pre)
        st_spec, st_shape = pl.BlockSpec((1, SUBLANES, dff2), lambda g, i: (g, 0, 0)), (G, SUBLANES, dff2)
        scratch = [pltpu.VMEM((tm, dff2 // 2), BF16), pltpu.VMEM((SUBLANES, dff2), F32)]
    else:
        in_specs += [row(dff2), row(dff2)]
        ins = (x, conv, od, of, wo, g2, wu, fw, wd, gf, pre[0], pre[1])
        st_spec, st_shape = row(dff2), (G, R, dff2)
        scratch = [pltpu.VMEM((tm, dff2 // 2), BF16)]
    kern = functools.partial(_outffn_kernel, tm=tm, period=period, prompt=prompt, final=final, chunk_w=chunk_w)
    return pl.pallas_call(
        kern, grid=(G, nt), in_specs=in_specs, out_specs=[row(D), st_spec],
        out_shape=[jax.ShapeDtypeStruct((G, R, D), F32), jax.ShapeDtypeStruct(st_shape, F32)],
        scratch_shapes=scratch, compiler_params=_cparams(2),
        name="outffn_prompt" if prompt else "outffn_sample")(*ins)


def _diff_decode_kernel(pt_ref, q_ref, *refs, pages, n_heads, lam_init, dec_seq):
    kp = refs[0:pages]
    vp = refs[pages:2 * pages]
    bp_ref, kn_ref, vn_ref, bn_ref, lam_ref, g_ref, o_ref, m_ref, l_ref, a_ref = refs[2 * pages:]
    c = pl.program_id(1)
    q = q_ref[0]
    rows = q.shape[0]

    def update(s, pv):
        m_old = m_ref[...]
        m_new = jnp.maximum(m_old, jnp.max(s, axis=-1, keepdims=True))
        alpha = jnp.exp(m_old - m_new)
        p = jnp.exp(s - m_new)
        l_ref[...] = alpha * l_ref[...] + jnp.sum(p, axis=-1, keepdims=True)
        a_ref[...] = alpha * a_ref[...] + pv(p.astype(BF16))
        m_ref[...] = m_new

    def per_head_pv(p, v_of_head):
        outs = []
        for h in range(n_heads):
            g = (h // 2) * 2 * SUBLANES
            res = _dot(p[g:g + 2 * SUBLANES, :], v_of_head(h))
            outs.append(res[(h % 2) * SUBLANES:(h % 2 + 1) * SUBLANES, :])
        return jnp.concatenate(outs, axis=0)

    @pl.when(c == 0)
    def _():
        m_ref[...] = jnp.full_like(m_ref, NEG_INF)
        l_ref[...] = jnp.zeros_like(l_ref)
        a_ref[...] = jnp.zeros_like(a_ref)
        vn = vn_ref[0]
        update(_dot_nt(q, kn_ref[0]) + bn_ref[...],
               lambda p: per_head_pv(p, lambda h: vn[:, h * LANES:(h + 1) * LANES]))

    kt = jnp.concatenate([r[0, 0] for r in kp], axis=1).astype(BF16)
    s = _dot(q, kt) + bp_ref[...]

    def v_of_head(h):
        return jnp.concatenate([r[0, 0, pl.ds(h, LANES, stride=n_heads), :] for r in vp], axis=0).astype(BF16)

    update(s, lambda p: per_head_pv(p, v_of_head))

    @pl.when(c == pl.num_programs(1) - 1)
    def _():
        lam = _lambda_of(lam_ref[...], lam_init)
        on = a_ref[...] / l_ref[...]
        o = on - lam * pltpu.roll(on, rows - dec_seq, 0)
        o_ref[0] = _rms(o, g_ref[...]) * (1.0 - lam_init)


def _diff_decode(pt_flat, qbd, dkt, dv4, layer, bias_past, knew, vnew, bias_new, lam_p, subln, *,
                 pages, n_pages, lam_init, dec_seq):
    DB, rows, dq = qbd.shape
    n_heads = dq // LANES
    nc = n_pages // pages
    page_spec = lambda jj: pl.BlockSpec(
        (1, 1, dq, LANES), lambda b, c, pt: (layer, pt[b * n_pages + c * pages + jj], 0, 0))
    in_specs = ([pl.BlockSpec((1, rows, dq), lambda b, c, pt: (b, 0, 0))]
                + [page_spec(jj) for jj in range(pages)] * 2
                + [pl.BlockSpec((rows, pages * LANES), lambda b, c, pt: (0, c)),
                   pl.BlockSpec((1,) + knew.shape[1:], lambda b, c, pt: (b, 0, 0)),
                   pl.BlockSpec((1,) + vnew.shape[1:], lambda b, c, pt: (b, 0, 0)),
                   pl.BlockSpec(bias_new.shape, lambda b, c, pt: (0, 0)),
                   pl.BlockSpec(lam_p.shape, lambda b, c, pt: (0, 0)),
                   pl.BlockSpec(subln.shape, lambda b, c, pt: (0, 0))])
    kern = functools.partial(_diff_decode_kernel, pages=pages, n_heads=n_heads, lam_init=lam_init, dec_seq=dec_seq)
    return pl.pallas_call(
        kern,
        grid_spec=pltpu.PrefetchScalarGridSpec(
            num_scalar_prefetch=1, grid=(DB, nc), in_specs=in_specs,
            out_specs=pl.BlockSpec((1, rows, LANES), lambda b, c, pt: (b, 0, 0)),
            scratch_shapes=[pltpu.VMEM((rows, 1), F32), pltpu.VMEM((rows, 1), F32), pltpu.VMEM((rows, LANES), F32)]),
        out_shape=jax.ShapeDtypeStruct((DB, rows, LANES), F32),
        compiler_params=_cparams(2), name="diff_attn_decode",
    )(pt_flat, qbd, *([dkt] * pages), *([dv4] * pages), bias_past, knew, vnew, bias_new, lam_p, subln)


def _fox_decode_kernel(pt_ref, q_ref, *refs, pages, n_heads, dec_seq):
    kp = refs[0:pages]
    vp = refs[pages:2 * pages]
    fp = refs[2 * pages:3 * pages]
    kn_ref, vn_ref, ln_ref, o_ref, m_ref, l_ref, a_ref, cq_ref, run_ref = refs[3 * pages:]
    c = pl.program_id(1)
    q = q_ref[0]
    rows = q.shape[0]

    def update(z, vt):
        m_old = m_ref[...]
        m_new = jnp.maximum(m_old, jnp.max(z, axis=-1, keepdims=True))
        alpha = jnp.exp(m_old - m_new)
        p = jnp.exp(z - m_new)
        l_ref[...] = alpha * l_ref[...] + jnp.sum(p, axis=-1, keepdims=True)
        a_ref[...] = alpha * a_ref[...] + _dot_nt(p.astype(BF16), vt)
        m_ref[...] = m_new

    @pl.when(c == 0)
    def _():
        m_ref[...] = jnp.full_like(m_ref, NEG_INF)
        l_ref[...] = jnp.zeros_like(l_ref)
        a_ref[...] = jnp.zeros_like(a_ref)
        run_ref[...] = jnp.zeros_like(run_ref)
        lane = lax.broadcasted_iota(I32, (rows, LANES), 1)
        trow = jnp.bitwise_and(lax.broadcasted_iota(I32, (rows, LANES), 0), dec_seq - 1)
        cnew = ln_ref[0]
        for s_ in (1, 2):
            cnew = cnew + jnp.where(lane >= s_, pltpu.roll(cnew, s_, 1), 0.0)
        cq = jnp.sum(jnp.where(lane == trow, cnew, 0.0), axis=-1, keepdims=True)
        cq_ref[...] = cq
        z = jnp.where(lane <= trow, _dot_nt(q, kn_ref[0]) + cq - cnew, NEG_INF)
        update(z, vn_ref[0])

    lane8 = lax.broadcasted_iota(I32, (SUBLANES, LANES), 1)
    later = run_ref[...]
    r_pages = [None] * pages
    for jj in reversed(range(pages)):
        y = _prefix_sum_lanes(fp[jj][0, 0], lane8)
        tot = jnp.broadcast_to(y[:, LANES - 1:LANES], (SUBLANES, LANES))
        r_pages[jj] = tot - y + later
        later = later + tot
    run_ref[...] = later
    r8 = jnp.concatenate(r_pages, axis=1)
    row_head = lax.shift_right_logical(lax.broadcasted_iota(I32, (rows, 1), 0), dec_seq.bit_length() - 1)
    r16 = jnp.zeros((rows, r8.shape[1]), F32)
    for h in range(n_heads):
        r16 = jnp.where(row_head == h, r8[h:h + 1, :], r16)
    kt = jnp.concatenate([r[0, 0] for r in kp], axis=1).astype(BF16)
    vt = jnp.concatenate([r[0, 0] for r in vp], axis=1).astype(BF16)
    update(_dot(q, kt) + r16 + cq_ref[...], vt)

    @pl.when(c == pl.num_programs(1) - 1)
    def _():
        o_ref[0] = a_ref[...] / l_ref[...]


def _fox_decode(pt_flat, qfbd, fkt, fvt, flt, layer, kfnew_t, vfnew_t, lnew, *, pages, n_pages, dec_seq):
    DB, rows, dfx = qfbd.shape
    n_heads = rows // dec_seq
    nc = n_pages // pages

    def page_spec(jj, r):
        return pl.BlockSpec((1, 1, r, LANES),
                            lambda b, c, pt: (layer, pt[b * n_pages + (nc - 1 - c) * pages + jj], 0, 0))

    in_specs = ([pl.BlockSpec((1, rows, dfx), lambda b, c, pt: (b, 0, 0))]
                + [page_spec(jj, dfx) for jj in range(pages)] * 2
                + [page_spec(jj, SUBLANES) for jj in range(pages)]
                + [pl.BlockSpec((1,) + kfnew_t.shape[1:], lambda b, c, pt: (b, 0, 0)),
                   pl.BlockSpec((1,) + vfnew_t.shape[1:], lambda b, c, pt: (b, 0, 0)),
                   pl.BlockSpec((1, rows, LANES), lambda b, c, pt: (b, 0, 0))])
    kern = functools.partial(_fox_decode_kernel, pages=pages, n_heads=n_heads, dec_seq=dec_seq)
    return pl.pallas_call(
        kern,
        grid_spec=pltpu.PrefetchScalarGridSpec(
            num_scalar_prefetch=1, grid=(DB, nc), in_specs=in_specs,
            out_specs=pl.BlockSpec((1, rows, dfx), lambda b, c, pt: (b, 0, 0)),
            scratch_shapes=[pltpu.VMEM((rows, 1), F32), pltpu.VMEM((rows, 1), F32), pltpu.VMEM((rows, dfx), F32),
                            pltpu.VMEM((rows, 1), F32), pltpu.VMEM((SUBLANES, LANES), F32)]),
        out_shape=jax.ShapeDtypeStruct((DB, rows, dfx), F32),
        compiler_params=_cparams(2), name="fox_attn_decode",
    )(pt_flat, qfbd, *([fkt] * pages), *([fvt] * pages), *([flt] * pages), kfnew_t, vfnew_t, lnew)


def _pad_rows(a, rows):
    return jnp.pad(a, ((0, rows - a.shape[0]),) + ((0, 0),) * (a.ndim - 1))


def _tile(n, pref):
    return pref if n % pref == 0 else n


def kernel(x_prompt, x_sample, cache_dk, cache_dv, cache_fk, cache_fv, cache_flogf, state_conv, state_ffn, page_table, rel_bias, norm1, w_in, b_f, conv_w, diff_lambda, subln, w_out, norm2, w_up, ffn_conv, w_down, norm_f):
    B, S, D = x_prompt.shape
    DB, T_NEW, _ = x_sample.shape
    depth, n_phys, page, n_da, _, da_qk = cache_dk.shape
    da_v = cache_dv.shape[-1]
    n_fox, fox_hd = cache_fk.shape[3], cache_fk.shape[4]
    cw = conv_w.shape[-1]
    dff = w_down.shape[1]
    n_pages = page_table.shape[1]
    past = n_pages * page
    dq, dfx = n_da * 2 * da_qk, n_fox * fox_hd
    assert page == LANES and 2 * da_qk == LANES and da_v == LANES and 2 * fox_hd == LANES
    assert n_fox % 2 == 0 and T_NEW & (T_NEW - 1) == 0 and T_NEW <= SUBLANES // 2
    assert (DB * T_NEW) % LANES == 0 and S % LANES == 0
    da_scale, fox_scale = da_qk ** -0.5, fox_hd ** -0.5
    t = _tile(S, 512)
    rs = DB * T_NEW
    pages = _tile(n_pages, 8)
    new_pad = 2 * SUBLANES
    ffn_chunk = _tile(dff, 256)

    dkt = jnp.transpose(cache_dk, (0, 1, 3, 4, 5, 2)).reshape(depth, n_phys, dq, page)
    dv4 = cache_dv.reshape(depth, n_phys, page * n_da, da_v)
    fkt = jnp.transpose(cache_fk, (0, 1, 3, 4, 2)).reshape(depth, n_phys, dfx, page)
    fvt = jnp.transpose(cache_fv, (0, 1, 3, 4, 2)).reshape(depth, n_phys, dfx, page)
    flt = jnp.pad(jnp.swapaxes(cache_flogf, 2, 3), ((0, 0), (0, 0), (0, SUBLANES - n_fox), (0, 0)))
    pt_flat = page_table.reshape(-1)

    dtiles, bias_past, bias_new = _bias_tiles(rel_bias, t=t, past=past, new_pad=new_pad, dec_seq=T_NEW)

    xp = x_prompt
    xs = x_sample.reshape(1, rs, D)
    zeros_c = jnp.zeros((B, SUBLANES, cw), F32)
    zeros_f = jnp.zeros((B, SUBLANES, 2 * dff), F32)
    eye_da = jnp.eye(n_da, dtype=BF16)
    pair_of_head = (jnp.arange(n_fox)[:, None] // 2 == jnp.arange(n_fox // 2)[None, :]).astype(BF16)
    st_p, st_s = [], []
    for l in range(depth):
        lam_init = 0.8 - 0.6 * math.exp(-0.3 * l)
        final = l == depth - 1
        wl = w_in[l]
        s0, s1 = 3 * cw, 3 * cw + dq
        wn = jnp.concatenate([wl[:, :s1], wl[:, s1 + dq:s1 + 2 * dq], wl[:, s1 + 2 * dq:s1 + 2 * dq + dfx]],
                             axis=1).astype(BF16)
        wt = jnp.concatenate([wl[:, s1:s1 + dq], wl[:, s1 + 2 * dq + dfx:s1 + 2 * dq + 3 * dfx]], axis=1).T.astype(BF16)
        wfl = _pad_rows(wl[:, s1 + 2 * dq + 3 * dfx:].T, 2 * SUBLANES).astype(BF16)
        bft = _pad_rows(b_f[l][:, None], SUBLANES)
        cwp = _pad_rows(conv_w[l], SUBLANES)
        fwp = _pad_rows(ffn_conv[l], SUBLANES)
        g1, g2, gf, gs = norm1[l][None], norm2[l][None], norm_f[None], subln[l][None]
        wo, wu, wd = w_out[l].astype(BF16), w_up[l].astype(BF16), w_down[l].astype(BF16)
        lam_p = diff_lambda[l]
        inproj = functools.partial(_inproj, da_scale=da_scale, fox_scale=fox_scale)

        (conv, cst, qz, v, vb, qfz, kt, ktb, kft, kftb, vft, vftb, lft, ct, ccol) = inproj(
            xp, g1, wn, wt, wfl, bft, cwp, zeros_c, tm=t, period=t, prompt=True)
        od = _diff_prompt(qz, ktb, vb, dtiles, lam_p, gs, t=t, lam_init=lam_init)
        of = _fox_prompt(qfz, kftb, vftb, ct, ccol, t=t)
        xp, fst = _outffn(xp, conv, od, of, wo, g2, wu, fwp, wd, gf, zeros_f,
                          tm=t, period=t, prompt=True, final=final, chunk_w=ffn_chunk)
        st_p.append((
            jnp.transpose(kt.reshape(B, n_da, 2, da_qk, S), (0, 4, 1, 2, 3)),
            v.reshape(B, S, n_da, da_v),
            jnp.transpose(kft.reshape(B, n_fox, fox_hd, S), (0, 3, 1, 2)),
            jnp.transpose(vft.reshape(B, n_fox, fox_hd, S), (0, 3, 1, 2)),
            jnp.swapaxes(lft[:, :n_fox, :], 1, 2),
            cst[:, SUBLANES - 2:, :], fst[:, SUBLANES - 2:, :]))

        pc = (jnp.repeat(state_conv[l][:, 0], T_NEW, axis=0)[None], jnp.repeat(state_conv[l][:, 1], T_NEW, axis=0)[None])
        pf = (jnp.repeat(state_ffn[l][:, 0], T_NEW, axis=0)[None], jnp.repeat(state_ffn[l][:, 1], T_NEW, axis=0)[None])
        (conv_s, gcu_s, qz_s, v_s, vb_s, qfz_s, kt_s, ktb_s, kft_s, kftb_s, vft_s, vftb_s, lft_s) = inproj(
            xs, g1, wn, wt, wfl, bft, cwp, pc, tm=rs, period=T_NEW, prompt=False)
        k_s = kt_s[0].T.reshape(DB, T_NEW, dq)
        kf_s = kft_s[0].T.reshape(DB, T_NEW, dfx)
        vf_s = vft_s[0].T.reshape(DB, T_NEW, dfx)
        lf_s = lft_s[0, :n_fox].T.reshape(DB, T_NEW, n_fox)
        pad_new = lambda a: jnp.pad(a, ((0, 0), (0, new_pad - T_NEW), (0, 0)))
        pad_lane = lambda a: jnp.pad(a, ((0, 0), (0, LANES - T_NEW), (0, 0)))
        q5 = jnp.transpose(qz_s.reshape(DB, T_NEW, n_da, 2, LANES), (0, 2, 3, 1, 4))
        qbd = (q5[:, :, :, :, None, :] * eye_da[None, :, None, None, :, None]).reshape(DB, n_da * 2 * T_NEW, dq)
        qf4 = jnp.transpose(qfz_s.reshape(DB, T_NEW, n_fox, LANES), (0, 2, 1, 3))
        qfbd = (qf4[:, :, :, None, :] * pair_of_head[None, :, None, :, None]).reshape(DB, n_fox * T_NEW, dfx)
        od_s = _diff_decode(pt_flat, qbd, dkt, dv4, l, bias_past, pad_new(k_s).astype(BF16),
                            pad_new(vb_s.reshape(DB, T_NEW, dq)), bias_new, lam_p, gs,
                            pages=pages, n_pages=n_pages, lam_init=lam_init, dec_seq=T_NEW)
        od_s = jnp.transpose(od_s.reshape(DB, n_da, 2, T_NEW, da_v)[:, :, 0], (0, 2, 1, 3)).reshape(1, rs, dq)
        lnew = jnp.pad(jnp.broadcast_to(jnp.swapaxes(lf_s, 1, 2)[:, :, None, :], (DB, n_fox, T_NEW, T_NEW)),
                       ((0, 0), (0, 0), (0, 0), (0, LANES - T_NEW))).reshape(DB, n_fox * T_NEW, LANES)
        of_s = _fox_decode(pt_flat, qfbd, fkt, fvt, flt, l,
                           pad_lane(kf_s).astype(BF16),
                           jnp.swapaxes(pad_lane(vf_s), 1, 2).astype(BF16),
                           lnew, pages=pages, n_pages=n_pages, dec_seq=T_NEW)
        of5 = of_s.reshape(DB, n_fox, T_NEW, n_fox, fox_hd)
        of_s = jnp.transpose(jnp.einsum('bhthd->bhtd', of5), (0, 2, 1, 3)).reshape(1, rs, dfx)
        xs, up_s = _outffn(xs, conv_s, od_s.astype(BF16), of_s.astype(BF16), wo, g2, wu, fwp, wd, gf, pf,
                           tm=rs, period=T_NEW, prompt=False, final=final, chunk_w=ffn_chunk)
        st_s.append((
            k_s.reshape(DB, T_NEW, n_da, 2, da_qk), v_s.reshape(DB, T_NEW, n_da, da_v),
            kf_s.reshape(DB, T_NEW, n_fox, fox_hd), vf_s.reshape(DB, T_NEW, n_fox, fox_hd), lf_s,
            gcu_s.reshape(DB, T_NEW, cw)[:, T_NEW - 2:], up_s.reshape(DB, T_NEW, 2 * dff)[:, T_NEW - 2:]))

    outs = [xp, xs.reshape(DB, T_NEW, D)]
    for j in range(7):
        outs.append(jnp.stack([s[j] for s in st_p]))
        outs.append(jnp.stack([s[j] for s in st_s]))
    return tuple(outs)
```

```python
import functools
import math

import jax
import jax.numpy as jnp
from jax import lax
from jax.experimental import pallas as pl
from jax.experimental.pallas import tpu as pltpu

F32, BF16, I32 = jnp.float32, jnp.bfloat16, jnp.int32
RMS_EPS = 1e-6
NEG_INF = -1e30
NUM_BUCKETS = 32
MAX_DISTANCE = 128
LANES = 128
SUBLANES = 8
VMEM_LIMIT_BYTES = 56 * 2**20
NT_DIMS = (((1,), (1,)), ((), ()))
LOG2E = math.log2(math.e)


def _cparams(n_axes):
    return pltpu.CompilerParams(dimension_semantics=("arbitrary",) * n_axes, vmem_limit_bytes=VMEM_LIMIT_BYTES)


def _dot(a, b):
    return jnp.dot(a, b, preferred_element_type=F32)


def _dot_nt(a, b):
    return lax.dot_general(a, b, NT_DIMS, preferred_element_type=F32)


def _rms(x, g):
    return x * lax.rsqrt(jnp.mean(x * x, axis=-1, keepdims=True) + RMS_EPS) * g


def _log_sigmoid(x):
    return jnp.minimum(x, 0.0) - jnp.log1p(jnp.exp(-jnp.abs(x)))


def _conv3(u, p0, p1, w_ref, cols, period):
    rows = u.shape[0]
    t = lax.broadcasted_iota(I32, (rows, 1), 0)
    if period < rows:
        t = jnp.bitwise_and(t, period - 1)
    sh1 = jnp.where(t == 0, p1, pltpu.roll(u, 1, 0))
    sh2 = jnp.where(t == 0, p0, jnp.where(t == 1, p1, pltpu.roll(u, 2, 0)))
    return sh2 * w_ref[0:1, cols] + sh1 * w_ref[1:2, cols] + u * w_ref[2:3, cols]


def _prefix_sum_lanes(y, lane):
    for s in (1, 2, 4, 8, 16, 32, 64):
        y = y + jnp.where(lane >= s, pltpu.roll(y, s, 1), 0.0)
    return y


def _inproj_kernel(*refs, tm, period, prompt, da_scale, fox_scale, n_da, n_pairs):
    if prompt:
        (x_ref, g_ref, wn_ref, wt_ref, wfl_ref, bf_ref, cw_ref, pref_ref,
         conv_ref, st_ref, qz_ref, v_ref, vb_ref, qfz_ref, kt_ref, ktb_ref, kft_ref, kftb_ref,
         vft_ref, vftb_ref, lft_ref, ct_ref, ccol_ref, carry_ref, ccarry_ref) = refs
    else:
        (x_ref, g_ref, wn_ref, wt_ref, wfl_ref, bf_ref, cw_ref, p0_ref, p1_ref,
         conv_ref, st_ref, qz_ref, v_ref, vb_ref, qfz_ref, kt_ref, ktb_ref, kft_ref, kftb_ref,
         vft_ref, vftb_ref, lft_ref) = refs
    i = pl.program_id(1)
    cw = cw_ref.shape[1]
    dq = n_da * LANES
    dfx = n_pairs * LANES
    h = _rms(x_ref[0], g_ref[...]).astype(BF16)

    ugc = _dot(h, wn_ref[:, 0:3 * cw])
    gcu = ugc[:, 2 * cw:3 * cw] * ugc[:, 0:cw]
    if prompt:
        @pl.when(i == 0)
        def _():
            carry_ref[...] = pref_ref[0]
            ccarry_ref[...] = jnp.zeros_like(ccarry_ref)
        p0, p1 = carry_ref[6:7, :], carry_ref[7:8, :]
    else:
        p0, p1 = p0_ref[0], p1_ref[0]
    cy = _conv3(gcu, p0, p1, cw_ref, slice(None), period)
    conv_ref[0] = (ugc[:, cw:2 * cw] * cy).astype(BF16)
    if prompt:
        carry_ref[...] = gcu[tm - SUBLANES:tm]
        st_ref[0] = gcu[tm - SUBLANES:tm]
    else:
        st_ref[0] = gcu

    lo = lax.broadcasted_iota(I32, (tm, LANES), 1) < (LANES // 2)
    c0 = 3 * cw
    q = _dot(h, wn_ref[:, c0:c0 + dq]) * da_scale
    for hh in range(n_da):
        qh = q[:, hh * LANES:(hh + 1) * LANES]
        qz_ref[0, :, 2 * hh * LANES:(2 * hh + 1) * LANES] = jnp.where(lo, qh, 0.0).astype(BF16)
        qz_ref[0, :, (2 * hh + 1) * LANES:(2 * hh + 2) * LANES] = jnp.where(lo, 0.0, qh).astype(BF16)
    c0 += dq
    v = _dot(h, wn_ref[:, c0:c0 + dq])
    v_ref[0] = v
    vb_ref[0] = v.astype(BF16)
    c0 += dq
    qf = _dot(h, wn_ref[:, c0:c0 + dfx]) * fox_scale
    for p in range(n_pairs):
        qp = qf[:, p * LANES:(p + 1) * LANES]
        qfz_ref[0, :, 2 * p * LANES:(2 * p + 1) * LANES] = jnp.where(lo, qp, 0.0).astype(BF16)
        qfz_ref[0, :, (2 * p + 1) * LANES:(2 * p + 2) * LANES] = jnp.where(lo, 0.0, qp).astype(BF16)

    kt = _dot_nt(wt_ref[0:dq, :], h)
    kt_ref[0] = kt
    kft = _dot_nt(wt_ref[dq:dq + dfx, :], h)
    kft_ref[0] = kft
    vft = _dot_nt(wt_ref[dq + dfx:dq + 2 * dfx, :], h)
    vft_ref[0] = vft
    if prompt:
        ktb_ref[0, 0] = kt.astype(BF16)
        kftb_ref[0, 0] = kft.astype(BF16)
        vftb_ref[0, 0] = vft.astype(BF16)
    else:
        ktb_ref[0] = kt.astype(BF16)
        kftb_ref[0] = kft.astype(BF16)
        vftb_ref[0] = vft.astype(BF16)
    lf = _log_sigmoid(_dot_nt(wfl_ref[...], h)[0:SUBLANES] + bf_ref[...])
    lft_ref[0] = lf

    if prompt:
        lane = lax.broadcasted_iota(I32, (SUBLANES, LANES), 1)
        carry = ccarry_ref[...]
        blocks = []
        for blk in range(tm // LANES):
            y = _prefix_sum_lanes(lf[:, blk * LANES:(blk + 1) * LANES], lane) + carry
            carry = jnp.broadcast_to(y[:, LANES - 1:LANES], (SUBLANES, LANES))
            blocks.append(y)
        ccarry_ref[...] = carry
        ct = jnp.concatenate(blocks, axis=1) * LOG2E
        ct_ref[0, 0] = ct
        ccol_ref[0] = jnp.concatenate([ct, jnp.zeros((LANES - SUBLANES, tm), F32)], axis=0).T


def _inproj(x, g1, wn, wt, wfl, bft, cwp, pre, *, tm, period, prompt, da_scale, fox_scale):
    G, R, D = x.shape
    nt = R // tm
    cw = cwp.shape[1]
    dq = (wn.shape[1] - 3 * cw) * 2 // 5
    dfx = dq // 2
    n_da, n_pairs = dq // LANES, dfx // LANES
    row = lambda c: pl.BlockSpec((1, tm, c), lambda g, i: (g, i, 0))
    colT = lambda r: pl.BlockSpec((1, r, tm), lambda g, i: (g, 0, i))
    chunkT = lambda r: pl.BlockSpec((1, 1, r, tm), lambda g, i: (g, i, 0, 0))
    full = lambda a: pl.BlockSpec(a.shape, lambda g, i: (0,) * a.ndim)
    in_specs = [row(D), full(g1), full(wn), full(wt), full(wfl), full(bft), full(cwp)]
    if prompt:
        in_specs += [pl.BlockSpec((1, SUBLANES, cw), lambda g, i: (g, 0, 0))]
        ins = (x, g1, wn, wt, wfl, bft, cwp, pre)
        st_spec, st_shape = pl.BlockSpec((1, SUBLANES, cw), lambda g, i: (g, 0, 0)), (G, SUBLANES, cw)
        tb = lambda r: (chunkT(r), jax.ShapeDtypeStruct((G, nt, r, tm), BF16))
    else:
        in_specs += [row(cw), row(cw)]
        ins = (x, g1, wn, wt, wfl, bft, cwp, pre[0], pre[1])
        st_spec, st_shape = row(cw), (G, R, cw)
        tb = lambda r: (colT(r), jax.ShapeDtypeStruct((G, r, R), BF16))
    outs = [
        (row(cw), jax.ShapeDtypeStruct((G, R, cw), BF16)),
        (st_spec, jax.ShapeDtypeStruct(st_shape, F32)),
        (row(2 * dq), jax.ShapeDtypeStruct((G, R, 2 * dq), BF16)),
        (row(dq), jax.ShapeDtypeStruct((G, R, dq), F32)),
        (row(dq), jax.ShapeDtypeStruct((G, R, dq), BF16)),
        (row(2 * dfx), jax.ShapeDtypeStruct((G, R, 2 * dfx), BF16)),
        (colT(dq), jax.ShapeDtypeStruct((G, dq, R), F32)),
        tb(dq),
        (colT(dfx), jax.ShapeDtypeStruct((G, dfx, R), F32)),
        tb(dfx),
        (colT(dfx), jax.ShapeDtypeStruct((G, dfx, R), F32)),
        tb(dfx),
        (colT(SUBLANES), jax.ShapeDtypeStruct((G, SUBLANES, R), F32)),
    ]
    scratch = []
    if prompt:
        outs += [
            (chunkT(SUBLANES), jax.ShapeDtypeStruct((G, nt, SUBLANES, tm), F32)),
            (row(LANES), jax.ShapeDtypeStruct((G, R, LANES), F32)),
        ]
        scratch = [pltpu.VMEM((SUBLANES, cw), F32), pltpu.VMEM((SUBLANES, LANES), F32)]
    kern = functools.partial(_inproj_kernel, tm=tm, period=period, prompt=prompt, da_scale=da_scale,
                             fox_scale=fox_scale, n_da=n_da, n_pairs=n_pairs)
    return pl.pallas_call(
        kern, grid=(G, nt), in_specs=in_specs, out_specs=[o[0] for o in outs],
        out_shape=[o[1] for o in outs], scratch_shapes=scratch, compiler_params=_cparams(2),
        name="inproj_prompt" if prompt else "inproj_sample")(*ins)


def _bias_of_distance(n, tab_ref, h, n_heads):
    max_exact = NUM_BUCKETS // 2
    nf = jnp.maximum(n, 1).astype(F32)
    large = max_exact + (jnp.log(nf / max_exact) / math.log(MAX_DISTANCE / max_exact)
                         * (NUM_BUCKETS - max_exact)).astype(I32)
    bucket = jnp.where(n < max_exact, n, jnp.minimum(large, NUM_BUCKETS - 1))
    far = tab_ref[(NUM_BUCKETS - 1) * n_heads + h]
    out = jnp.zeros(n.shape, F32)
    for b in range(NUM_BUCKETS - 1):
        out = jnp.where(bucket == b, (tab_ref[b * n_heads + h] - far) * LOG2E, out)
    return out


def _bias_kernel(tab_ref, d_ref, bp_ref, bn_ref, *, t, past, n_heads, new_pad, dec_seq):
    r = lax.broadcasted_iota(I32, (t, t), 0)
    c = lax.broadcasted_iota(I32, (t, t), 1)
    rp = lax.broadcasted_iota(I32, (SUBLANES, past), 0)
    kp = lax.broadcasted_iota(I32, (SUBLANES, past), 1)
    rn = lax.broadcasted_iota(I32, (SUBLANES, new_pad), 0)
    kn = lax.broadcasted_iota(I32, (SUBLANES, new_pad), 1)
    tp = jnp.bitwise_and(rp, dec_seq - 1)
    tn = jnp.bitwise_and(rn, dec_seq - 1)
    for h in range(n_heads):
        d_ref[h, 0] = jnp.where(r >= c, _bias_of_distance(jnp.maximum(r - c, 0), tab_ref, h, n_heads), NEG_INF)
        d_ref[h, 1] = _bias_of_distance(t + r - c, tab_ref, h, n_heads)
        bp_ref[h * SUBLANES:(h + 1) * SUBLANES, :] = _bias_of_distance(past + tp - kp, tab_ref, h, n_heads)
        bn_ref[h * SUBLANES:(h + 1) * SUBLANES, :] = jnp.where(
            kn <= tn, _bias_of_distance(jnp.maximum(tn - kn, 0), tab_ref, h, n_heads), NEG_INF)


def _bias_tiles(rel_bias, *, t, past, new_pad, dec_seq):
    n_heads = rel_bias.shape[1]
    kern = functools.partial(_bias_kernel, t=t, past=past, n_heads=n_heads, new_pad=new_pad, dec_seq=dec_seq)
    return pl.pallas_call(
        kern,
        in_specs=[pl.BlockSpec(memory_space=pltpu.SMEM)],
        out_specs=[pl.BlockSpec(memory_space=pltpu.VMEM)] * 3,
        out_shape=[jax.ShapeDtypeStruct((n_heads, 2, t, t), F32),
                   jax.ShapeDtypeStruct((n_heads * SUBLANES, past), F32),
                   jax.ShapeDtypeStruct((n_heads * SUBLANES, new_pad), F32)],
        compiler_params=pltpu.CompilerParams(vmem_limit_bytes=VMEM_LIMIT_BYTES),
        name="rel_bias_tiles")(rel_bias.reshape(-1))


def _lambda_of(lp, lam_init):
    a = jnp.sum(lp[0:1] * lp[1:2], axis=-1, keepdims=True)
    b = jnp.sum(lp[2:3] * lp[3:4], axis=-1, keepdims=True)
    return jnp.exp(a) - jnp.exp(b) + lam_init


def _flash_update(s, m_ref, acc_ref, pv, row_off=None):
    blocks = [s[:, c * LANES:(c + 1) * LANES] for c in range(s.shape[1] // LANES)]
    rmax = jnp.max(functools.reduce(jnp.maximum, blocks), axis=-1, keepdims=True)
    m_old = m_ref[...]
    m_new = jnp.maximum(m_old, rmax if row_off is None else rmax + row_off)
    alpha = jnp.exp2(m_old - m_new)
    shift = m_new if row_off is None else m_new - row_off
    p = jnp.concatenate([jnp.exp2(b - shift).astype(BF16) for b in blocks], axis=1)
    acc_ref[...] = jnp.concatenate([alpha, alpha], axis=1) * acc_ref[...] + pv(p)
    m_ref[...] = m_new


def _diff_prompt_kernel(q_ref, k_ref, v_ref, d_ref, lam_ref, g_ref, o_ref, m1, a1, m2, a2, *, t, lam_init):
    i = pl.program_id(2)
    for m, a in ((m1, a1), (m2, a2)):
        m[...] = jnp.full_like(m, NEG_INF)
        a[...] = jnp.zeros_like(a)
    q1, q2 = q_ref[0, :, 0:LANES], q_ref[0, :, LANES:2 * LANES]

    def chunk(j, bias):
        kt = k_ref[0, j]
        v = v_ref[0, pl.ds(pl.multiple_of(j * t, t), t), :]
        v1 = jnp.concatenate([v, jnp.ones_like(v)], axis=1)
        for q, m, a in ((q1, m1, a1), (q2, m2, a2)):
            s = _dot(q, kt)
            if bias is not None:
                s = s + bias
            _flash_update(s, m, a, lambda p: _dot(p, v1))

    def far(j, carry):
        chunk(j, None)
        return carry

    lax.fori_loop(0, jnp.maximum(i - 1, 0), far, 0)

    @pl.when(i >= 1)
    def _():
        chunk(i - 1, d_ref[0, 1])

    chunk(i, d_ref[0, 0])
    lam = _lambda_of(lam_ref[...], lam_init)
    o = a1[:, 0:LANES] / a1[:, LANES:] - lam * (a2[:, 0:LANES] / a2[:, LANES:])
    o_ref[0] = (_rms(o, g_ref[...]) * (1.0 - lam_init)).astype(BF16)


def _diff_prompt(qz, ktb, vb, dtiles, lam_p, subln, *, t, lam_init):
    B, S, _ = qz.shape
    H = dtiles.shape[0]
    nq = S // t
    kern = functools.partial(_diff_prompt_kernel, t=t, lam_init=lam_init)
    return pl.pallas_call(
        kern, grid=(B, H, nq),
        in_specs=[pl.BlockSpec((1, t, 2 * LANES), lambda b, h, i: (b, i, h)),
                  pl.BlockSpec((1, nq, LANES, t), lambda b, h, i: (b, 0, h, 0)),
                  pl.BlockSpec((1, S, LANES), lambda b, h, i: (b, 0, h)),
                  pl.BlockSpec((1, 2, t, t), lambda b, h, i: (h, 0, 0, 0)),
                  pl.BlockSpec(lam_p.shape, lambda b, h, i: (0, 0)),
                  pl.BlockSpec(subln.shape, lambda b, h, i: (0, 0))],
        out_specs=pl.BlockSpec((1, t, LANES), lambda b, h, i: (b, i, h)),
        out_shape=jax.ShapeDtypeStruct((B, S, H * LANES), BF16),
        scratch_shapes=[pltpu.VMEM((t, LANES), F32), pltpu.VMEM((t, 2 * LANES), F32)] * 2,
        compiler_params=_cparams(3), name="diff_attn_prompt")(qz, ktb, vb, dtiles, lam_p, subln)


def _fox_prompt_kernel(q_ref, k_ref, v_ref, c_ref, cc_ref, o_ref, m0, a0, m1, a1, *, t):
    pr = pl.program_id(1)
    i = pl.program_id(2)
    state = ((m0, a0), (m1, a1))
    for m, a in state:
        m[...] = jnp.full_like(m, NEG_INF)
        a[...] = jnp.zeros_like(a)
    lane = lax.broadcasted_iota(I32, (t, LANES), 1)
    ccol = cc_ref[0]
    heads = []
    for e in range(2):
        hsel = lane == (2 * pr + e)
        cq = jnp.sum(jnp.where(hsel, ccol, 0.0), axis=-1, keepdims=True)
        cbase = cq[0:1, :]
        heads.append((q_ref[0, :, e * LANES:(e + 1) * LANES], jnp.broadcast_to(cq - cbase, (t, LANES)), cbase, state[e]))

    def chunk(j, diag):
        kt = k_ref[0, j]
        vt = v_ref[0, j]
        vt1 = jnp.concatenate([vt, jnp.ones_like(vt)], axis=0)
        crow = c_ref[0, j]
        crow_idx = lax.broadcasted_iota(I32, crow.shape, 0)
        for e, (q, cqr, cbase, (m, a)) in enumerate(heads):
            ck = jnp.sum(jnp.where(crow_idx == 2 * pr + e, crow, 0.0), axis=0, keepdims=True)
            z = _dot(q, kt) - (ck - cbase)
            if diag:
                r = lax.broadcasted_iota(I32, (t, t), 0)
                c = lax.broadcasted_iota(I32, (t, t), 1)
                z = jnp.where(r >= c, z, NEG_INF)
            _flash_update(z, m, a, lambda p: _dot_nt(p, vt1), row_off=cqr)

    def far(j, carry):
        chunk(j, False)
        return carry

    lax.fori_loop(0, i, far, 0)
    chunk(i, True)
    o_ref[0] = jnp.where(lane < LANES // 2, a0[:, 0:LANES] / a0[:, LANES:], a1[:, 0:LANES] / a1[:, LANES:]).astype(BF16)


def _fox_prompt(qfz, kftb, vftb, ct, ccol, *, t):
    B, S, _ = qfz.shape
    nq = S // t
    n_pairs = kftb.shape[2] // LANES
    kern = functools.partial(_fox_prompt_kernel, t=t)
    return pl.pallas_call(
        kern, grid=(B, n_pairs, nq),
        in_specs=[pl.BlockSpec((1, t, 2 * LANES), lambda b, p, i: (b, i, p)),
                  pl.BlockSpec((1, nq, LANES, t), lambda b, p, i: (b, 0, p, 0)),
                  pl.BlockSpec((1, nq, LANES, t), lambda b, p, i: (b, 0, p, 0)),
                  pl.BlockSpec((1, nq, SUBLANES, t), lambda b, p, i: (b, 0, 0, 0)),
                  pl.BlockSpec((1, t, LANES), lambda b, p, i: (b, i, 0))],
        out_specs=pl.BlockSpec((1, t, LANES), lambda b, p, i: (b, i, p)),
        out_shape=jax.ShapeDtypeStruct((B, S, n_pairs * LANES), BF16),
        scratch_shapes=[pltpu.VMEM((t, LANES), F32), pltpu.VMEM((t, 2 * LANES), F32)] * 2,
        compiler_params=_cparams(3), name="fox_attn_prompt")(qfz, kftb, vftb, ct, ccol)


def _outffn_kernel(*refs, tm, period, prompt, final, chunk_w):
    if prompt:
        (x_ref, cv_ref, od_ref, of_ref, wo_ref, g2_ref, wu_ref, fw_ref, wd_ref, gf_ref, pref_ref,
         xo_ref, st_ref, act_ref, carry_ref) = refs
    else:
        (x_ref, cv_ref, od_ref, of_ref, wo_ref, g2_ref, wu_ref, fw_ref, wd_ref, gf_ref, p0_ref, p1_ref,
         xo_ref, st_ref, act_ref) = refs
    i = pl.program_id(1)
    cw, dw = cv_ref.shape[2], od_ref.shape[2]
    dff = wd_ref.shape[0]
    x1 = (x_ref[0] + _dot(cv_ref[0], wo_ref[0:cw, :]) + _dot(od_ref[0], wo_ref[cw:cw + dw, :])
          + _dot(of_ref[0], wo_ref[cw + dw:, :]))
    xn = _rms(x1, g2_ref[...]).astype(BF16)
    if prompt:
        @pl.when(i == 0)
        def _():
            carry_ref[...] = pref_ref[0]
    for c in range(dff // chunk_w):
        halves = []
        for base in (c * chunk_w, dff + c * chunk_w):
            cols = slice(base, base + chunk_w)
            up = _dot(xn, wu_ref[:, cols])
            if prompt:
                p0, p1 = carry_ref[6:7, cols], carry_ref[7:8, cols]
            else:
                p0, p1 = p0_ref[0, :, cols], p1_ref[0, :, cols]
            halves.append(_conv3(up, p0, p1, fw_ref, cols, period))
            if prompt:
                carry_ref[:, cols] = up[tm - SUBLANES:tm]
                st_ref[0, :, cols] = up[tm - SUBLANES:tm]
            else:
                st_ref[0, :, cols] = up
        val, gate = halves
        act = gate * (1.0 / (1.0 + jnp.exp(-gate))) * val
        act_ref[:, c * chunk_w:(c + 1) * chunk_w] = act.astype(BF16)
    x2 = x1 + _dot(act_ref[...], wd_ref[...])
    xo_ref[0] = _rms(x2, gf_ref[...]) if final else x2


def _outffn(x, conv, od, of, wo, g2, wu, fw, wd, gf, pre, *, tm, period, prompt, final, chunk_w):
    G, R, D = x.shape
    nt = R // tm
    dff2 = wu.shape[1]
    row = lambda c: pl.BlockSpec((1, tm, c), lambda g, i: (g, i, 0))
    once = lambda a: pl.BlockSpec(a.shape, lambda g, i: (0,) * a.ndim, pipeline_mode=pl.Buffered(1))
    in_specs = [row(D), row(conv.shape[2]), row(od.shape[2]), row(of.shape[2]),
                once(wo), once(g2), once(wu), once(fw), once(wd), once(gf)]
    if prompt:
        in_specs += [pl.BlockSpec((1, SUBLANES, dff2), lambda g, i: (g, 0, 0))]
        ins = (x, conv, od, of, wo, g2, wu, fw, wd, gf, pre)
        st_spec, st_shape = pl.BlockSpec((1, SUBLANES, dff2), lambda g, i: (g, 0, 0)), (G, SUBLANES, dff2)
        scratch = [pltpu.VMEM((tm, dff2 // 2), BF16), pltpu.VMEM((SUBLANES, dff2), F32)]
    else:
        in_specs += [row(dff2), row(dff2)]
        ins = (x, conv, od, of, wo, g2, wu, fw, wd, gf, pre[0], pre[1])
        st_spec, st_shape = row(dff2), (G, R, dff2)
        scratch = [pltpu.VMEM((tm, dff2 // 2), BF16)]
    kern = functools.partial(_outffn_kernel, tm=tm, period=period, prompt=prompt, final=final, chunk_w=chunk_w)
    return pl.pallas_call(
        kern, grid=(G, nt), in_specs=in_specs, out_specs=[row(D), st_spec],
        out_shape=[jax.ShapeDtypeStruct((G, R, D), F32), jax.ShapeDtypeStruct(st_shape, F32)],
        scratch_shapes=scratch, compiler_params=_cparams(2),
        name="outffn_prompt" if prompt else "outffn_sample")(*ins)


def _diff_decode_kernel(pt_ref, q_ref, *refs, pages, n_heads, lam_init, dec_seq):
    kp = refs[0:pages]
    vp = refs[pages:2 * pages]
    bp_ref, kn_ref, vn_ref, bn_ref, lam_ref, g_ref, o_ref, m_ref, l_ref, a_ref = refs[2 * pages:]
    c = pl.program_id(1)
    q = q_ref[0]
    rows = q.shape[0]

    def update(s, pv):
        m_old = m_ref[...]
        m_new = jnp.maximum(m_old, jnp.max(s, axis=-1, keepdims=True))
        alpha = jnp.exp2(m_old - m_new)
        p = jnp.exp2(s - m_new)
        l_ref[...] = alpha * l_ref[...] + jnp.sum(p, axis=-1, keepdims=True)
        a_ref[...] = alpha * a_ref[...] + pv(p.astype(BF16))
        m_ref[...] = m_new

    def per_head_pv(p, v_of_head):
        outs = []
        for h in range(n_heads):
            g = (h // 2) * 2 * SUBLANES
            res = _dot(p[g:g + 2 * SUBLANES, :], v_of_head(h))
            outs.append(res[(h % 2) * SUBLANES:(h % 2 + 1) * SUBLANES, :])
        return jnp.concatenate(outs, axis=0)

    @pl.when(c == 0)
    def _():
        m_ref[...] = jnp.full_like(m_ref, NEG_INF)
        l_ref[...] = jnp.zeros_like(l_ref)
        a_ref[...] = jnp.zeros_like(a_ref)
        vn = vn_ref[0]
        update(_dot_nt(q, kn_ref[0]) + bn_ref[...],
               lambda p: per_head_pv(p, lambda h: vn[:, h * LANES:(h + 1) * LANES]))

    kt = jnp.concatenate([r[0, 0] for r in kp], axis=1).astype(BF16)
    s = _dot(q, kt) + bp_ref[...]

    def v_of_head(h):
        return jnp.concatenate([r[0, 0, pl.ds(h, LANES, stride=n_heads), :] for r in vp], axis=0).astype(BF16)

    update(s, lambda p: per_head_pv(p, v_of_head))

    @pl.when(c == pl.num_programs(1) - 1)
    def _():
        lam = _lambda_of(lam_ref[...], lam_init)
        on = a_ref[...] / l_ref[...]
        o = on - lam * pltpu.roll(on, rows - dec_seq, 0)
        o_ref[0] = _rms(o, g_ref[...]) * (1.0 - lam_init)


def _fox_decode_kernel(pt_ref, q_ref, *refs, pages, n_heads, dec_seq):
    kp = refs[0:pages]
    vp = refs[pages:2 * pages]
    fp = refs[2 * pages:3 * pages]
    kn_ref, vn_ref, ln_ref, o_ref, m_ref, l_ref, a_ref, cq_ref, run_ref = refs[3 * pages:]
    c = pl.program_id(1)
    q = q_ref[0]
    rows = q.shape[0]

    def update(z, vt):
        m_old = m_ref[...]
        m_new = jnp.maximum(m_old, jnp.max(z, axis=-1, keepdims=True))
        alpha = jnp.exp2(m_old - m_new)
        p = jnp.exp2(z - m_new)
        l_ref[...] = alpha * l_ref[...] + jnp.sum(p, axis=-1, keepdims=True)
        a_ref[...] = alpha * a_ref[...] + _dot_nt(p.astype(BF16), vt)
        m_ref[...] = m_new

    @pl.when(c == 0)
    def _():
        m_ref[...] = jnp.full_like(m_ref, NEG_INF)
        l_ref[...] = jnp.zeros_like(l_ref)
        a_ref[...] = jnp.zeros_like(a_ref)
        run_ref[...] = jnp.zeros_like(run_ref)
        lane = lax.broadcasted_iota(I32, (rows, LANES), 1)
        trow = jnp.bitwise_and(lax.broadcasted_iota(I32, (rows, LANES), 0), dec_seq - 1)
        cnew = ln_ref[0] * LOG2E
        for s_ in (1, 2):
            cnew = cnew + jnp.where(lane >= s_, pltpu.roll(cnew, s_, 1), 0.0)
        cq = jnp.sum(jnp.where(lane == trow, cnew, 0.0), axis=-1, keepdims=True)
        cq_ref[...] = cq
        z = jnp.where(lane <= trow, _dot_nt(q, kn_ref[0]) + cq - cnew, NEG_INF)
        update(z, vn_ref[0])

    lane8 = lax.broadcasted_iota(I32, (SUBLANES, LANES), 1)
    later = run_ref[...]
    r_pages = [None] * pages
    for jj in reversed(range(pages)):
        y = _prefix_sum_lanes(fp[jj][0, 0] * LOG2E, lane8)
        tot = jnp.broadcast_to(y[:, LANES - 1:LANES], (SUBLANES, LANES))
        r_pages[jj] = tot - y + later
        later = later + tot
    run_ref[...] = later
    r8 = jnp.concatenate(r_pages, axis=1)
    row_head = lax.shift_right_logical(lax.broadcasted_iota(I32, (rows, 1), 0), dec_seq.bit_length() - 1)
    r16 = jnp.zeros((rows, r8.shape[1]), F32)
    for h in range(n_heads):
        r16 = jnp.where(row_head == h, r8[h:h + 1, :], r16)
    kt = jnp.concatenate([r[0, 0] for r in kp], axis=1).astype(BF16)
    vt = jnp.concatenate([r[0, 0] for r in vp], axis=1).astype(BF16)
    update(_dot(q, kt) + r16 + cq_ref[...], vt)

    @pl.when(c == pl.num_programs(1) - 1)
    def _():
        o_ref[0] = a_ref[...] / l_ref[...]


def _decode_kernel(pt_ref, *refs, pages, n_da, n_fox, lam_init, dec_seq):
    n_d, n_f = 1 + 2 * pages + 6, 1 + 3 * pages + 3
    d_in, f_in = refs[:n_d], refs[n_d:n_d + n_f]
    od_ref, of_ref = refs[n_d + n_f:n_d + n_f + 2]
    sc = refs[n_d + n_f + 2:]
    _diff_decode_kernel(pt_ref, *d_in, od_ref, *sc[0:3], pages=pages, n_heads=n_da, lam_init=lam_init, dec_seq=dec_seq)
    _fox_decode_kernel(pt_ref, *f_in, of_ref, *sc[3:8], pages=pages, n_heads=n_fox, dec_seq=dec_seq)


def _decode(pt_flat, layer, qbd, dkt, dv4, bias_past, knew, vnew, bias_new, lam_p, subln,
            qfbd, fkt, fvt, flt, kfnew, vfnew_t, lnew, *, pages, n_pages, lam_init, dec_seq):
    DB, drows, dq = qbd.shape
    _, frows, dfx = qfbd.shape
    nc = n_pages // pages
    per_seq = lambda a: pl.BlockSpec((1,) + a.shape[1:], lambda b, c, pt: (b, 0, 0))
    shared = lambda a: pl.BlockSpec(a.shape, lambda b, c, pt: (0, 0))

    def page_spec(jj, r, newest_first):
        chunk = (lambda c: nc - 1 - c) if newest_first else (lambda c: c)
        return pl.BlockSpec((1, 1, r, LANES),
                            lambda b, c, pt: (layer, pt[b * n_pages + chunk(c) * pages + jj], 0, 0))

    in_specs = ([per_seq(qbd)]
                + [page_spec(jj, dq, False) for jj in range(pages)] * 2
                + [pl.BlockSpec((drows, pages * LANES), lambda b, c, pt: (0, c)),
                   per_seq(knew), per_seq(vnew), shared(bias_new), shared(lam_p), shared(subln)]
                + [per_seq(qfbd)]
                + [page_spec(jj, dfx, True) for jj in range(pages)] * 2
                + [page_spec(jj, SUBLANES, True) for jj in range(pages)]
                + [per_seq(kfnew), per_seq(vfnew_t), per_seq(lnew)])
    kern = functools.partial(_decode_kernel, pages=pages, n_da=dq // LANES, n_fox=frows // dec_seq,
                             lam_init=lam_init, dec_seq=dec_seq)
    return pl.pallas_call(
        kern,
        grid_spec=pltpu.PrefetchScalarGridSpec(
            num_scalar_prefetch=1, grid=(DB, nc), in_specs=in_specs,
            out_specs=[pl.BlockSpec((1, drows, LANES), lambda b, c, pt: (b, 0, 0)),
                       pl.BlockSpec((1, frows, dfx), lambda b, c, pt: (b, 0, 0))],
            scratch_shapes=[pltpu.VMEM((drows, 1), F32), pltpu.VMEM((drows, 1), F32), pltpu.VMEM((drows, LANES), F32),
                            pltpu.VMEM((frows, 1), F32), pltpu.VMEM((frows, 1), F32), pltpu.VMEM((frows, dfx), F32),
                            pltpu.VMEM((frows, 1), F32), pltpu.VMEM((SUBLANES, LANES), F32)]),
        out_shape=[jax.ShapeDtypeStruct((DB, drows, LANES), F32), jax.ShapeDtypeStruct((DB, frows, dfx), F32)],
        compiler_params=_cparams(2), name="attn_decode",
    )(pt_flat, qbd, *([dkt] * pages), *([dv4] * pages), bias_past, knew, vnew, bias_new, lam_p, subln,
      qfbd, *([fkt] * pages), *([fvt] * pages), *([flt] * pages), kfnew, vfnew_t, lnew)


def _pad_rows(a, rows):
    return jnp.pad(a, ((0, rows - a.shape[0]),) + ((0, 0),) * (a.ndim - 1))


def _tile(n, pref):
    return pref if n % pref == 0 else n


def kernel(x_prompt, x_sample, cache_dk, cache_dv, cache_fk, cache_fv, cache_flogf, state_conv, state_ffn, page_table, rel_bias, norm1, w_in, b_f, conv_w, diff_lambda, subln, w_out, norm2, w_up, ffn_conv, w_down, norm_f):
    B, S, D = x_prompt.shape
    DB, T_NEW, _ = x_sample.shape
    depth, n_phys, page, n_da, _, da_qk = cache_dk.shape
    da_v = cache_dv.shape[-1]
    n_fox, fox_hd = cache_fk.shape[3], cache_fk.shape[4]
    cw = conv_w.shape[-1]
    dff = w_down.shape[1]
    n_pages = page_table.shape[1]
    past = n_pages * page
    dq, dfx = n_da * 2 * da_qk, n_fox * fox_hd
    assert page == LANES and 2 * da_qk == LANES and da_v == LANES and 2 * fox_hd == LANES
    assert n_fox % 2 == 0 and T_NEW & (T_NEW - 1) == 0 and T_NEW <= SUBLANES // 2
    assert (DB * T_NEW) % LANES == 0 and S % LANES == 0
    da_scale, fox_scale = da_qk ** -0.5, fox_hd ** -0.5
    t = _tile(S, 512)
    rs = DB * T_NEW
    pages = _tile(n_pages, 16)
    new_pad = 2 * SUBLANES
    ffn_chunk = _tile(dff, 256)

    dkt = jnp.transpose(cache_dk, (0, 1, 3, 4, 5, 2)).reshape(depth, n_phys, dq, page)
    dv4 = cache_dv.reshape(depth, n_phys, page * n_da, da_v)
    fkt = jnp.transpose(cache_fk, (0, 1, 3, 4, 2)).reshape(depth, n_phys, dfx, page)
    fvt = jnp.transpose(cache_fv, (0, 1, 3, 4, 2)).reshape(depth, n_phys, dfx, page)
    flt = jnp.pad(jnp.swapaxes(cache_flogf, 2, 3), ((0, 0), (0, 0), (0, SUBLANES - n_fox), (0, 0)))
    pt_flat = page_table.reshape(-1)

    dtiles, bias_past, bias_new = _bias_tiles(rel_bias, t=t, past=past, new_pad=new_pad, dec_seq=T_NEW)

    xp = x_prompt
    xs = x_sample.reshape(1, rs, D)
    zeros_c = jnp.zeros((B, SUBLANES, cw), F32)
    zeros_f = jnp.zeros((B, SUBLANES, 2 * dff), F32)
    eye_da = jnp.eye(n_da, dtype=BF16)
    pair_of_head = (jnp.arange(n_fox)[:, None] // 2 == jnp.arange(n_fox // 2)[None, :]).astype(BF16)
    st_p, st_s = [], []
    for l in range(depth):
        lam_init = 0.8 - 0.6 * math.exp(-0.3 * l)
        final = l == depth - 1
        wl = w_in[l]
        s0, s1 = 3 * cw, 3 * cw + dq
        wn = jnp.concatenate([wl[:, :s1], wl[:, s1 + dq:s1 + 2 * dq], wl[:, s1 + 2 * dq:s1 + 2 * dq + dfx]],
                             axis=1).astype(BF16)
        wt = jnp.concatenate([wl[:, s1:s1 + dq], wl[:, s1 + 2 * dq + dfx:s1 + 2 * dq + 3 * dfx]], axis=1).T.astype(BF16)
        wfl = _pad_rows(wl[:, s1 + 2 * dq + 3 * dfx:].T, 2 * SUBLANES).astype(BF16)
        bft = _pad_rows(b_f[l][:, None], SUBLANES)
        cwp = _pad_rows(conv_w[l], SUBLANES)
        fwp = _pad_rows(ffn_conv[l], SUBLANES)
        g1, g2, gf, gs = norm1[l][None], norm2[l][None], norm_f[None], subln[l][None]
        wo, wu, wd = w_out[l].astype(BF16), w_up[l].astype(BF16), w_down[l].astype(BF16)
        lam_p = diff_lambda[l]
        inproj = functools.partial(_inproj, da_scale=da_scale * LOG2E, fox_scale=fox_scale * LOG2E)

        (conv, cst, qz, v, vb, qfz, kt, ktb, kft, kftb, vft, vftb, lft, ct, ccol) = inproj(
            xp, g1, wn, wt, wfl, bft, cwp, zeros_c, tm=t, period=t, prompt=True)
        od = _diff_prompt(qz, ktb, vb, dtiles, lam_p, gs, t=t, lam_init=lam_init)
        of = _fox_prompt(qfz, kftb, vftb, ct, ccol, t=t)
        xp, fst = _outffn(xp, conv, od, of, wo, g2, wu, fwp, wd, gf, zeros_f,
                          tm=t, period=t, prompt=True, final=final, chunk_w=ffn_chunk)
        st_p.append((
            jnp.transpose(kt.reshape(B, n_da, 2, da_qk, S), (0, 4, 1, 2, 3)),
            v.reshape(B, S, n_da, da_v),
            jnp.transpose(kft.reshape(B, n_fox, fox_hd, S), (0, 3, 1, 2)),
            jnp.transpose(vft.reshape(B, n_fox, fox_hd, S), (0, 3, 1, 2)),
            jnp.swapaxes(lft[:, :n_fox, :], 1, 2),
            cst[:, SUBLANES - 2:, :], fst[:, SUBLANES - 2:, :]))

        pc = (jnp.repeat(state_conv[l][:, 0], T_NEW, axis=0)[None], jnp.repeat(state_conv[l][:, 1], T_NEW, axis=0)[None])
        pf = (jnp.repeat(state_ffn[l][:, 0], T_NEW, axis=0)[None], jnp.repeat(state_ffn[l][:, 1], T_NEW, axis=0)[None])
        (conv_s, gcu_s, qz_s, v_s, vb_s, qfz_s, kt_s, ktb_s, kft_s, kftb_s, vft_s, vftb_s, lft_s) = inproj(
            xs, g1, wn, wt, wfl, bft, cwp, pc, tm=rs, period=T_NEW, prompt=False)
        k_s = kt_s[0].T.reshape(DB, T_NEW, dq)
        kf_s = kft_s[0].T.reshape(DB, T_NEW, dfx)
        vf_s = vft_s[0].T.reshape(DB, T_NEW, dfx)
        lf_s = lft_s[0, :n_fox].T.reshape(DB, T_NEW, n_fox)
        pad_new = lambda a: jnp.pad(a, ((0, 0), (0, new_pad - T_NEW), (0, 0)))
        pad_lane = lambda a: jnp.pad(a, ((0, 0), (0, LANES - T_NEW), (0, 0)))
        q5 = jnp.transpose(qz_s.reshape(DB, T_NEW, n_da, 2, LANES), (0, 2, 3, 1, 4))
        qbd = (q5[:, :, :, :, None, :] * eye_da[None, :, None, None, :, None]).reshape(DB, n_da * 2 * T_NEW, dq)
        qf4 = jnp.transpose(qfz_s.reshape(DB, T_NEW, n_fox, LANES), (0, 2, 1, 3))
        qfbd = (qf4[:, :, :, None, :] * pair_of_head[None, :, None, :, None]).reshape(DB, n_fox * T_NEW, dfx)
        lnew = jnp.pad(jnp.broadcast_to(jnp.swapaxes(lf_s, 1, 2)[:, :, None, :], (DB, n_fox, T_NEW, T_NEW)),
                       ((0, 0), (0, 0), (0, 0), (0, LANES - T_NEW))).reshape(DB, n_fox * T_NEW, LANES)
        od_s, of_s = _decode(pt_flat, l, qbd, dkt, dv4, bias_past, pad_new(k_s).astype(BF16),
                             pad_new(vb_s.reshape(DB, T_NEW, dq)), bias_new, lam_p, gs,
                             qfbd, fkt, fvt, flt, pad_lane(kf_s).astype(BF16),
                             jnp.swapaxes(pad_lane(vf_s), 1, 2).astype(BF16), lnew,
                             pages=pages, n_pages=n_pages, lam_init=lam_init, dec_seq=T_NEW)
        od_s = jnp.transpose(od_s.reshape(DB, n_da, 2, T_NEW, da_v)[:, :, 0], (0, 2, 1, 3)).reshape(1, rs, dq)
        of5 = of_s.reshape(DB, n_fox, T_NEW, n_fox, fox_hd)
        of_s = jnp.transpose(jnp.einsum('bhthd->bhtd', of5), (0, 2, 1, 3)).reshape(1, rs, dfx)
        xs, up_s = _outffn(xs, conv_s, od_s.astype(BF16), of_s.astype(BF16), wo, g2, wu, fwp, wd, gf, pf,
                           tm=rs, period=T_NEW, prompt=False, final=final, chunk_w=ffn_chunk)
        st_s.append((
            k_s.reshape(DB, T_NEW, n_da, 2, da_qk), v_s.reshape(DB, T_NEW, n_da, da_v),
            kf_s.reshape(DB, T_NEW, n_fox, fox_hd), vf_s.reshape(DB, T_NEW, n_fox, fox_hd), lf_s,
            gcu_s.reshape(DB, T_NEW, cw)[:, T_NEW - 2:], up_s.reshape(DB, T_NEW, 2 * dff)[:, T_NEW - 2:]))

    outs = [xp, xs.reshape(DB, T_NEW, D)]
    for j in range(7):
        outs.append(jnp.stack([s[j] for s in st_p]))
        outs.append(jnp.stack([s[j] for s in st_s]))
    return tuple(outs)
```

```python
import functools
import math

import jax
import jax.numpy as jnp
from jax import lax
from jax.experimental import pallas as pl
from jax.experimental.pallas import tpu as pltpu

F32, BF16, I32 = jnp.float32, jnp.bfloat16, jnp.int32
RMS_EPS = 1e-6
NEG_INF = -1e30
NUM_BUCKETS = 32
MAX_DISTANCE = 128
LANES = 128
SUBLANES = 8
VMEM_LIMIT_BYTES = 56 * 2**20
NT_DIMS = (((1,), (1,)), ((), ()))
LOG2E = math.log2(math.e)


def _cparams(n_axes):
    return pltpu.CompilerParams(dimension_semantics=("arbitrary",) * n_axes, vmem_limit_bytes=VMEM_LIMIT_BYTES)


def _dot(a, b):
    return jnp.dot(a, b, preferred_element_type=F32)


def _dot_nt(a, b):
    return lax.dot_general(a, b, NT_DIMS, preferred_element_type=F32)


def _rms(x, g):
    return x * lax.rsqrt(jnp.mean(x * x, axis=-1, keepdims=True) + RMS_EPS) * g


def _log_sigmoid(x):
    return jnp.minimum(x, 0.0) - jnp.log1p(jnp.exp(-jnp.abs(x)))


def _conv3(u, p0, p1, w_ref, cols, period):
    rows = u.shape[0]
    t = lax.broadcasted_iota(I32, (rows, 1), 0)
    if period < rows:
        t = jnp.bitwise_and(t, period - 1)
    sh1 = jnp.where(t == 0, p1, pltpu.roll(u, 1, 0))
    sh2 = jnp.where(t == 0, p0, jnp.where(t == 1, p1, pltpu.roll(u, 2, 0)))
    return sh2 * w_ref[0:1, cols] + sh1 * w_ref[1:2, cols] + u * w_ref[2:3, cols]


def _prefix_sum_lanes(y, lane):
    for s in (1, 2, 4, 8, 16, 32, 64):
        y = y + jnp.where(lane >= s, pltpu.roll(y, s, 1), 0.0)
    return y


def _inproj_kernel(*refs, tm, period, prompt, da_scale, fox_scale, n_da, n_pairs):
    if prompt:
        (x_ref, g_ref, wn_ref, wt_ref, wfl_ref, bf_ref, cw_ref, pref_ref,
         conv_ref, st_ref, qz_ref, v_ref, vb_ref, qfz_ref, kt_ref, ktb_ref, kft_ref, kftb_ref,
         vft_ref, vftb_ref, lft_ref, ct_ref, ccol_ref, carry_ref, ccarry_ref) = refs
    else:
        (x_ref, g_ref, wn_ref, wt_ref, wfl_ref, bf_ref, cw_ref, p0_ref, p1_ref,
         conv_ref, st_ref, qz_ref, v_ref, vb_ref, qfz_ref, kt_ref, ktb_ref, kft_ref, kftb_ref,
         vft_ref, vftb_ref, lft_ref) = refs
    i = pl.program_id(1)
    cw = cw_ref.shape[1]
    dq = n_da * LANES
    dfx = n_pairs * LANES
    h = _rms(x_ref[0], g_ref[...]).astype(BF16)

    ugc = _dot(h, wn_ref[:, 0:3 * cw])
    gcu = ugc[:, 2 * cw:3 * cw] * ugc[:, 0:cw]
    if prompt:
        @pl.when(i == 0)
        def _():
            carry_ref[...] = pref_ref[0]
            ccarry_ref[...] = jnp.zeros_like(ccarry_ref)
        p0, p1 = carry_ref[6:7, :], carry_ref[7:8, :]
    else:
        p0, p1 = p0_ref[0], p1_ref[0]
    cy = _conv3(gcu, p0, p1, cw_ref, slice(None), period)
    conv_ref[0] = (ugc[:, cw:2 * cw] * cy).astype(BF16)
    if prompt:
        carry_ref[...] = gcu[tm - SUBLANES:tm]
        st_ref[0] = gcu[tm - SUBLANES:tm]
    else:
        st_ref[0] = gcu

    lo = lax.broadcasted_iota(I32, (tm, LANES), 1) < (LANES // 2)
    c0 = 3 * cw
    q = _dot(h, wn_ref[:, c0:c0 + dq]) * da_scale
    for hh in range(n_da):
        qh = q[:, hh * LANES:(hh + 1) * LANES]
        qz_ref[0, :, 2 * hh * LANES:(2 * hh + 1) * LANES] = jnp.where(lo, qh, 0.0).astype(BF16)
        qz_ref[0, :, (2 * hh + 1) * LANES:(2 * hh + 2) * LANES] = jnp.where(lo, 0.0, qh).astype(BF16)
    c0 += dq
    v = _dot(h, wn_ref[:, c0:c0 + dq])
    v_ref[0] = v
    vb_ref[0] = v.astype(BF16)
    c0 += dq
    qf = _dot(h, wn_ref[:, c0:c0 + dfx]) * fox_scale
    for p in range(n_pairs):
        qp = qf[:, p * LANES:(p + 1) * LANES]
        qfz_ref[0, :, 2 * p * LANES:(2 * p + 1) * LANES] = jnp.where(lo, qp, 0.0).astype(BF16)
        qfz_ref[0, :, (2 * p + 1) * LANES:(2 * p + 2) * LANES] = jnp.where(lo, 0.0, qp).astype(BF16)

    kt = _dot_nt(wt_ref[0:dq, :], h)
    kt_ref[0] = kt
    kft = _dot_nt(wt_ref[dq:dq + dfx, :], h)
    kft_ref[0] = kft
    vft = _dot_nt(wt_ref[dq + dfx:dq + 2 * dfx, :], h)
    vft_ref[0] = vft
    if prompt:
        ktb_ref[0, 0] = kt.astype(BF16)
        kftb_ref[0, 0] = kft.astype(BF16)
        vftb_ref[0, 0] = vft.astype(BF16)
    else:
        ktb_ref[0] = kt.astype(BF16)
        kftb_ref[0] = kft.astype(BF16)
        vftb_ref[0] = vft.astype(BF16)
    lf = _log_sigmoid(_dot_nt(wfl_ref[...], h)[0:SUBLANES] + bf_ref[...])
    lft_ref[0] = lf

    if prompt:
        lane = lax.broadcasted_iota(I32, (SUBLANES, LANES), 1)
        carry = ccarry_ref[...]
        blocks = []
        for blk in range(tm // LANES):
            y = _prefix_sum_lanes(lf[:, blk * LANES:(blk + 1) * LANES], lane) + carry
            carry = jnp.broadcast_to(y[:, LANES - 1:LANES], (SUBLANES, LANES))
            blocks.append(y)
        ccarry_ref[...] = carry
        ct = jnp.concatenate(blocks, axis=1) * LOG2E
        ct_ref[0, 0] = ct
        ccol_ref[0] = jnp.concatenate([ct, jnp.zeros((LANES - SUBLANES, tm), F32)], axis=0).T


def _inproj(x, g1, wn, wt, wfl, bft, cwp, pre, *, tm, period, prompt, da_scale, fox_scale):
    G, R, D = x.shape
    nt = R // tm
    cw = cwp.shape[1]
    dq = (wn.shape[1] - 3 * cw) * 2 // 5
    dfx = dq // 2
    n_da, n_pairs = dq // LANES, dfx // LANES
    row = lambda c: pl.BlockSpec((1, tm, c), lambda g, i: (g, i, 0))
    colT = lambda r: pl.BlockSpec((1, r, tm), lambda g, i: (g, 0, i))
    chunkT = lambda r: pl.BlockSpec((1, 1, r, tm), lambda g, i: (g, i, 0, 0))
    full = lambda a: pl.BlockSpec(a.shape, lambda g, i: (0,) * a.ndim)
    in_specs = [row(D), full(g1), full(wn), full(wt), full(wfl), full(bft), full(cwp)]
    if prompt:
        in_specs += [pl.BlockSpec((1, SUBLANES, cw), lambda g, i: (g, 0, 0))]
        ins = (x, g1, wn, wt, wfl, bft, cwp, pre)
        st_spec, st_shape = pl.BlockSpec((1, SUBLANES, cw), lambda g, i: (g, 0, 0)), (G, SUBLANES, cw)
        tb = lambda r: (chunkT(r), jax.ShapeDtypeStruct((G, nt, r, tm), BF16))
    else:
        in_specs += [row(cw), row(cw)]
        ins = (x, g1, wn, wt, wfl, bft, cwp, pre[0], pre[1])
        st_spec, st_shape = row(cw), (G, R, cw)
        tb = lambda r: (colT(r), jax.ShapeDtypeStruct((G, r, R), BF16))
    outs = [
        (row(cw), jax.ShapeDtypeStruct((G, R, cw), BF16)),
        (st_spec, jax.ShapeDtypeStruct(st_shape, F32)),
        (row(2 * dq), jax.ShapeDtypeStruct((G, R, 2 * dq), BF16)),
        (row(dq), jax.ShapeDtypeStruct((G, R, dq), F32)),
        (row(dq), jax.ShapeDtypeStruct((G, R, dq), BF16)),
        (row(2 * dfx), jax.ShapeDtypeStruct((G, R, 2 * dfx), BF16)),
        (colT(dq), jax.ShapeDtypeStruct((G, dq, R), F32)),
        tb(dq),
        (colT(dfx), jax.ShapeDtypeStruct((G, dfx, R), F32)),
        tb(dfx),
        (colT(dfx), jax.ShapeDtypeStruct((G, dfx, R), F32)),
        tb(dfx),
        (colT(SUBLANES), jax.ShapeDtypeStruct((G, SUBLANES, R), F32)),
    ]
    scratch = []
    if prompt:
        outs += [
            (chunkT(SUBLANES), jax.ShapeDtypeStruct((G, nt, SUBLANES, tm), F32)),
            (row(LANES), jax.ShapeDtypeStruct((G, R, LANES), F32)),
        ]
        scratch = [pltpu.VMEM((SUBLANES, cw), F32), pltpu.VMEM((SUBLANES, LANES), F32)]
    kern = functools.partial(_inproj_kernel, tm=tm, period=period, prompt=prompt, da_scale=da_scale,
                             fox_scale=fox_scale, n_da=n_da, n_pairs=n_pairs)
    return pl.pallas_call(
        kern, grid=(G, nt), in_specs=in_specs, out_specs=[o[0] for o in outs],
        out_shape=[o[1] for o in outs], scratch_shapes=scratch, compiler_params=_cparams(2),
        name="inproj_prompt" if prompt else "inproj_sample")(*ins)


def _bias_of_distance(n, tab_ref, h, n_heads):
    max_exact = NUM_BUCKETS // 2
    nf = jnp.maximum(n, 1).astype(F32)
    large = max_exact + (jnp.log(nf / max_exact) / math.log(MAX_DISTANCE / max_exact)
                         * (NUM_BUCKETS - max_exact)).astype(I32)
    bucket = jnp.where(n < max_exact, n, jnp.minimum(large, NUM_BUCKETS - 1))
    far = tab_ref[(NUM_BUCKETS - 1) * n_heads + h]
    out = jnp.zeros(n.shape, F32)
    for b in range(NUM_BUCKETS - 1):
        out = jnp.where(bucket == b, (tab_ref[b * n_heads + h] - far) * LOG2E, out)
    return out


def _bias_kernel(tab_ref, d_ref, bp_ref, bn_ref, *, t, past, n_heads, new_pad, dec_seq):
    r = lax.broadcasted_iota(I32, (t, t), 0)
    c = lax.broadcasted_iota(I32, (t, t), 1)
    rp = lax.broadcasted_iota(I32, (SUBLANES, past), 0)
    kp = lax.broadcasted_iota(I32, (SUBLANES, past), 1)
    rn = lax.broadcasted_iota(I32, (SUBLANES, new_pad), 0)
    kn = lax.broadcasted_iota(I32, (SUBLANES, new_pad), 1)
    tp = jnp.bitwise_and(rp, dec_seq - 1)
    tn = jnp.bitwise_and(rn, dec_seq - 1)
    for h in range(n_heads):
        d_ref[h, 0] = jnp.where(r >= c, _bias_of_distance(jnp.maximum(r - c, 0), tab_ref, h, n_heads), NEG_INF)
        d_ref[h, 1] = _bias_of_distance(t + r - c, tab_ref, h, n_heads)
        bp_ref[h * SUBLANES:(h + 1) * SUBLANES, :] = _bias_of_distance(past + tp - kp, tab_ref, h, n_heads)
        bn_ref[h * SUBLANES:(h + 1) * SUBLANES, :] = jnp.where(
            kn <= tn, _bias_of_distance(jnp.maximum(tn - kn, 0), tab_ref, h, n_heads), NEG_INF)


def _bias_tiles(rel_bias, *, t, past, new_pad, dec_seq):
    n_heads = rel_bias.shape[1]
    kern = functools.partial(_bias_kernel, t=t, past=past, n_heads=n_heads, new_pad=new_pad, dec_seq=dec_seq)
    return pl.pallas_call(
        kern,
        in_specs=[pl.BlockSpec(memory_space=pltpu.SMEM)],
        out_specs=[pl.BlockSpec(memory_space=pltpu.VMEM)] * 3,
        out_shape=[jax.ShapeDtypeStruct((n_heads, 2, t, t), F32),
                   jax.ShapeDtypeStruct((n_heads * SUBLANES, past), F32),
                   jax.ShapeDtypeStruct((n_heads * SUBLANES, new_pad), F32)],
        compiler_params=pltpu.CompilerParams(vmem_limit_bytes=VMEM_LIMIT_BYTES),
        name="rel_bias_tiles")(rel_bias.reshape(-1))


def _lambda_of(lp, lam_init):
    a = jnp.sum(lp[0:1] * lp[1:2], axis=-1, keepdims=True)
    b = jnp.sum(lp[2:3] * lp[3:4], axis=-1, keepdims=True)
    return jnp.exp(a) - jnp.exp(b) + lam_init


def _flash_update(s, m_ref, acc_ref, pv, row_off=None):
    blocks = [s[:, c * LANES:(c + 1) * LANES] for c in range(s.shape[1] // LANES)]
    rmax = jnp.max(functools.reduce(jnp.maximum, blocks), axis=-1, keepdims=True)
    m_old = m_ref[...]
    m_new = jnp.maximum(m_old, rmax if row_off is None else rmax + row_off)
    alpha = jnp.exp2(m_old - m_new)
    shift = m_new if row_off is None else m_new - row_off
    p = jnp.concatenate([jnp.exp2(b - shift).astype(BF16) for b in blocks], axis=1)
    acc_ref[...] = jnp.concatenate([alpha, alpha], axis=1) * acc_ref[...] + pv(p)
    m_ref[...] = m_new


def _run_chunks(n, chunk):
    def quad(jq, carry):
        for u in range(4):
            chunk(4 * jq + u)
        return carry

    lax.fori_loop(0, lax.shift_right_logical(n, 2), quad, 0)
    base = jnp.bitwise_and(n, -4)

    @pl.when(jnp.bitwise_and(n, 2) != 0)
    def _():
        chunk(base)
        chunk(base + 1)

    @pl.when(jnp.bitwise_and(n, 1) != 0)
    def _():
        chunk(n - 1)


def _diff_prompt_kernel(q_ref, k_ref, v_ref, d_ref, lam_ref, g_ref, o_ref, m1, a1, m2, a2, *, t, lam_init):
    i = pl.program_id(2)
    for m, a in ((m1, a1), (m2, a2)):
        m[...] = jnp.full_like(m, NEG_INF)
        a[...] = jnp.zeros_like(a)
    q1, q2 = q_ref[0, :, 0:LANES], q_ref[0, :, LANES:2 * LANES]

    def chunk(j, bias):
        kt = k_ref[0, j]
        v = v_ref[0, pl.ds(pl.multiple_of(j * t, t), t), :]
        v1 = jnp.concatenate([v, jnp.ones_like(v)], axis=1)
        for q, m, a in ((q1, m1, a1), (q2, m2, a2)):
            s = _dot(q, kt)
            if bias is not None:
                s = s + bias
            _flash_update(s, m, a, lambda p: _dot(p, v1))

    _run_chunks(jnp.maximum(i - 1, 0), lambda j: chunk(j, None))

    @pl.when(i >= 1)
    def _():
        chunk(i - 1, d_ref[0, 1])
        chunk(i, d_ref[0, 0])

    @pl.when(i == 0)
    def _():
        chunk(i, d_ref[0, 0])

    lam = _lambda_of(lam_ref[...], lam_init)
    o = a1[:, 0:LANES] / a1[:, LANES:] - lam * (a2[:, 0:LANES] / a2[:, LANES:])
    o_ref[0] = (_rms(o, g_ref[...]) * (1.0 - lam_init)).astype(BF16)


def _diff_prompt(qz, ktb, vb, dtiles, lam_p, subln, *, t, lam_init):
    B, S, _ = qz.shape
    H = dtiles.shape[0]
    nq = S // t
    kern = functools.partial(_diff_prompt_kernel, t=t, lam_init=lam_init)
    return pl.pallas_call(
        kern, grid=(B, H, nq),
        in_specs=[pl.BlockSpec((1, t, 2 * LANES), lambda b, h, i: (b, i, h)),
                  pl.BlockSpec((1, nq, LANES, t), lambda b, h, i: (b, 0, h, 0)),
                  pl.BlockSpec((1, S, LANES), lambda b, h, i: (b, 0, h)),
                  pl.BlockSpec((1, 2, t, t), lambda b, h, i: (h, 0, 0, 0)),
                  pl.BlockSpec(lam_p.shape, lambda b, h, i: (0, 0)),
                  pl.BlockSpec(subln.shape, lambda b, h, i: (0, 0))],
        out_specs=pl.BlockSpec((1, t, LANES), lambda b, h, i: (b, i, h)),
        out_shape=jax.ShapeDtypeStruct((B, S, H * LANES), BF16),
        scratch_shapes=[pltpu.VMEM((t, LANES), F32), pltpu.VMEM((t, 2 * LANES), F32)] * 2,
        compiler_params=_cparams(3), name="diff_attn_prompt")(qz, ktb, vb, dtiles, lam_p, subln)


def _fox_prompt_kernel(q_ref, k_ref, v_ref, c_ref, cc_ref, o_ref, m0, a0, m1, a1, *, t):
    pr = pl.program_id(1)
    i = pl.program_id(2)
    state = ((m0, a0), (m1, a1))
    for m, a in state:
        m[...] = jnp.full_like(m, NEG_INF)
        a[...] = jnp.zeros_like(a)
    lane = lax.broadcasted_iota(I32, (t, LANES), 1)
    ccol = cc_ref[0]
    heads = []
    for e in range(2):
        hsel = lane == (2 * pr + e)
        cq = jnp.sum(jnp.where(hsel, ccol, 0.0), axis=-1, keepdims=True)
        cbase = cq[0:1, :]
        heads.append((q_ref[0, :, e * LANES:(e + 1) * LANES], jnp.broadcast_to(cq - cbase, (t, LANES)), cbase, state[e]))

    def chunk(j, diag):
        kt = k_ref[0, j]
        vt = v_ref[0, j]
        vt1 = jnp.concatenate([vt, jnp.ones_like(vt)], axis=0)
        crow = c_ref[0, j]
        crow_idx = lax.broadcasted_iota(I32, crow.shape, 0)
        for e, (q, cqr, cbase, (m, a)) in enumerate(heads):
            ck = jnp.sum(jnp.where(crow_idx == 2 * pr + e, crow, 0.0), axis=0, keepdims=True)
            z = _dot(q, kt) - (ck - cbase)
            if diag:
                r = lax.broadcasted_iota(I32, (t, t), 0)
                c = lax.broadcasted_iota(I32, (t, t), 1)
                z = jnp.where(r >= c, z, NEG_INF)
            _flash_update(z, m, a, lambda p: _dot_nt(p, vt1), row_off=cqr)

    _run_chunks(i, lambda j: chunk(j, False))
    chunk(i, True)
    o_ref[0] = jnp.where(lane < LANES // 2, a0[:, 0:LANES] / a0[:, LANES:], a1[:, 0:LANES] / a1[:, LANES:]).astype(BF16)


def _fox_prompt(qfz, kftb, vftb, ct, ccol, *, t):
    B, S, _ = qfz.shape
    nq = S // t
    n_pairs = kftb.shape[2] // LANES
    kern = functools.partial(_fox_prompt_kernel, t=t)
    return pl.pallas_call(
        kern, grid=(B, n_pairs, nq),
        in_specs=[pl.BlockSpec((1, t, 2 * LANES), lambda b, p, i: (b, i, p)),
                  pl.BlockSpec((1, nq, LANES, t), lambda b, p, i: (b, 0, p, 0)),
                  pl.BlockSpec((1, nq, LANES, t), lambda b, p, i: (b, 0, p, 0)),
                  pl.BlockSpec((1, nq, SUBLANES, t), lambda b, p, i: (b, 0, 0, 0)),
                  pl.BlockSpec((1, t, LANES), lambda b, p, i: (b, i, 0))],
        out_specs=pl.BlockSpec((1, t, LANES), lambda b, p, i: (b, i, p)),
        out_shape=jax.ShapeDtypeStruct((B, S, n_pairs * LANES), BF16),
        scratch_shapes=[pltpu.VMEM((t, LANES), F32), pltpu.VMEM((t, 2 * LANES), F32)] * 2,
        compiler_params=_cparams(3), name="fox_attn_prompt")(qfz, kftb, vftb, ct, ccol)


def _outffn_kernel(*refs, tm, period, prompt, final, chunk_w):
    if prompt:
        (x_ref, cv_ref, od_ref, of_ref, wo_ref, g2_ref, wu_ref, fw_ref, wd_ref, gf_ref, pref_ref,
         xo_ref, st_ref, act_ref, carry_ref) = refs
    else:
        (x_ref, cv_ref, od_ref, of_ref, wo_ref, g2_ref, wu_ref, fw_ref, wd_ref, gf_ref, p0_ref, p1_ref,
         xo_ref, st_ref, act_ref) = refs
    i = pl.program_id(1)
    cw, dw = cv_ref.shape[2], od_ref.shape[2]
    dff = wd_ref.shape[0]
    x1 = (x_ref[0] + _dot(cv_ref[0], wo_ref[0:cw, :]) + _dot(od_ref[0], wo_ref[cw:cw + dw, :])
          + _dot(of_ref[0], wo_ref[cw + dw:, :]))
    xn = _rms(x1, g2_ref[...]).astype(BF16)
    if prompt:
        @pl.when(i == 0)
        def _():
            carry_ref[...] = pref_ref[0]
    for c in range(dff // chunk_w):
        halves = []
        for base in (c * chunk_w, dff + c * chunk_w):
            cols = slice(base, base + chunk_w)
            up = _dot(xn, wu_ref[:, cols])
            if prompt:
                p0, p1 = carry_ref[6:7, cols], carry_ref[7:8, cols]
            else:
                p0, p1 = p0_ref[0, :, cols], p1_ref[0, :, cols]
            halves.append(_conv3(up, p0, p1, fw_ref, cols, period))
            if prompt:
                carry_ref[:, cols] = up[tm - SUBLANES:tm]
                st_ref[0, :, cols] = up[tm - SUBLANES:tm]
            else:
                st_ref[0, :, cols] = up
        val, gate = halves
        act = gate * (1.0 / (1.0 + jnp.exp(-gate))) * val
        act_ref[:, c * chunk_w:(c + 1) * chunk_w] = act.astype(BF16)
    x2 = x1 + _dot(act_ref[...], wd_ref[...])
    xo_ref[0] = _rms(x2, gf_ref[...]) if final else x2


def _outffn(x, conv, od, of, wo, g2, wu, fw, wd, gf, pre, *, tm, period, prompt, final, chunk_w):
    G, R, D = x.shape
    nt = R // tm
    dff2 = wu.shape[1]
    row = lambda c: pl.BlockSpec((1, tm, c), lambda g, i: (g, i, 0))
    once = lambda a: pl.BlockSpec(a.shape, lambda g, i: (0,) * a.ndim, pipeline_mode=pl.Buffered(1))
    in_specs = [row(D), row(conv.shape[2]), row(od.shape[2]), row(of.shape[2]),
                once(wo), once(g2), once(wu), once(fw), once(wd), once(gf)]
    if prompt:
        in_specs += [pl.BlockSpec((1, SUBLANES, dff2), lambda g, i: (g, 0, 0))]
        ins = (x, conv, od, of, wo, g2, wu, fw, wd, gf, pre)
        st_spec, st_shape = pl.BlockSpec((1, SUBLANES, dff2), lambda g, i: (g, 0, 0)), (G, SUBLANES, dff2)
        scratch = [pltpu.VMEM((tm, dff2 // 2), BF16), pltpu.VMEM((SUBLANES, dff2), F32)]
    else:
        in_specs += [row(dff2), row(dff2)]
        ins = (x, conv, od, of, wo, g2, wu, fw, wd, gf, pre[0], pre[1])
        st_spec, st_shape = row(dff2), (G, R, dff2)
        scratch = [pltpu.VMEM((tm, dff2 // 2), BF16)]
    kern = functools.partial(_outffn_kernel, tm=tm, period=period, prompt=prompt, final=final, chunk_w=chunk_w)
    return pl.pallas_call(
        kern, grid=(G, nt), in_specs=in_specs, out_specs=[row(D), st_spec],
        out_shape=[jax.ShapeDtypeStruct((G, R, D), F32), jax.ShapeDtypeStruct(st_shape, F32)],
        scratch_shapes=scratch, compiler_params=_cparams(2),
        name="outffn_prompt" if prompt else "outffn_sample")(*ins)


def _diff_decode_kernel(pt_ref, q_ref, *refs, pages, n_heads, lam_init, dec_seq):
    kp = refs[0:pages]
    vp = refs[pages:2 * pages]
    bp_ref, kn_ref, vn_ref, bn_ref, lam_ref, g_ref, o_ref, m_ref, l_ref, a_ref = refs[2 * pages:]
    c = pl.program_id(1)
    q = q_ref[0]
    rows = q.shape[0]

    def update(s, pv):
        m_old = m_ref[...]
        m_new = jnp.maximum(m_old, jnp.max(s, axis=-1, keepdims=True))
        alpha = jnp.exp2(m_old - m_new)
        p = jnp.exp2(s - m_new)
        l_ref[...] = alpha * l_ref[...] + jnp.sum(p, axis=-1, keepdims=True)
        a_ref[...] = alpha * a_ref[...] + pv(p.astype(BF16))
        m_ref[...] = m_new

    def per_head_pv(p, v_of_head):
        outs = []
        for h in range(n_heads):
            g = (h // 2) * 2 * SUBLANES
            res = _dot(p[g:g + 2 * SUBLANES, :], v_of_head(h))
            outs.append(res[(h % 2) * SUBLANES:(h % 2 + 1) * SUBLANES, :])
        return jnp.concatenate(outs, axis=0)

    @pl.when(c == 0)
    def _():
        m_ref[...] = jnp.full_like(m_ref, NEG_INF)
        l_ref[...] = jnp.zeros_like(l_ref)
        a_ref[...] = jnp.zeros_like(a_ref)
        vn = vn_ref[0]
        update(_dot_nt(q, kn_ref[0]) + bn_ref[...],
               lambda p: per_head_pv(p, lambda h: vn[:, h * LANES:(h + 1) * LANES]))

    kt = jnp.concatenate([r[0, 0] for r in kp], axis=1).astype(BF16)
    s = _dot(q, kt) + bp_ref[...]

    def v_of_head(h):
        return jnp.concatenate([r[0, 0, pl.ds(h, LANES, stride=n_heads), :] for r in vp], axis=0).astype(BF16)

    update(s, lambda p: per_head_pv(p, v_of_head))

    @pl.when(c == pl.num_programs(1) - 1)
    def _():
        lam = _lambda_of(lam_ref[...], lam_init)
        on = a_ref[...] / l_ref[...]
        o = on - lam * pltpu.roll(on, rows - dec_seq, 0)
        o_ref[0] = _rms(o, g_ref[...]) * (1.0 - lam_init)


def _fox_decode_kernel(pt_ref, q_ref, *refs, pages, n_heads, dec_seq):
    kp = refs[0:pages]
    vp = refs[pages:2 * pages]
    fp = refs[2 * pages:3 * pages]
    kn_ref, vn_ref, ln_ref, o_ref, m_ref, l_ref, a_ref, cq_ref, run_ref = refs[3 * pages:]
    c = pl.program_id(1)
    q = q_ref[0]
    rows = q.shape[0]

    def update(z, vt):
        m_old = m_ref[...]
        m_new = jnp.maximum(m_old, jnp.max(z, axis=-1, keepdims=True))
        alpha = jnp.exp2(m_old - m_new)
        p = jnp.exp2(z - m_new)
        l_ref[...] = alpha * l_ref[...] + jnp.sum(p, axis=-1, keepdims=True)
        a_ref[...] = alpha * a_ref[...] + _dot_nt(p.astype(BF16), vt)
        m_ref[...] = m_new

    @pl.when(c == 0)
    def _():
        m_ref[...] = jnp.full_like(m_ref, NEG_INF)
        l_ref[...] = jnp.zeros_like(l_ref)
        a_ref[...] = jnp.zeros_like(a_ref)
        run_ref[...] = jnp.zeros_like(run_ref)
        lane = lax.broadcasted_iota(I32, (rows, LANES), 1)
        trow = jnp.bitwise_and(lax.broadcasted_iota(I32, (rows, LANES), 0), dec_seq - 1)
        cnew = ln_ref[0] * LOG2E
        for s_ in (1, 2):
            cnew = cnew + jnp.where(lane >= s_, pltpu.roll(cnew, s_, 1), 0.0)
        cq = jnp.sum(jnp.where(lane == trow, cnew, 0.0), axis=-1, keepdims=True)
        cq_ref[...] = cq
        z = jnp.where(lane <= trow, _dot_nt(q, kn_ref[0]) + cq - cnew, NEG_INF)
        update(z, vn_ref[0])

    lane8 = lax.broadcasted_iota(I32, (SUBLANES, LANES), 1)
    later = run_ref[...]
    r_pages = [None] * pages
    for jj in reversed(range(pages)):
        y = _prefix_sum_lanes(fp[jj][0, 0] * LOG2E, lane8)
        tot = jnp.broadcast_to(y[:, LANES - 1:LANES], (SUBLANES, LANES))
        r_pages[jj] = tot - y + later
        later = later + tot
    run_ref[...] = later
    r8 = jnp.concatenate(r_pages, axis=1)
    row_head = lax.shift_right_logical(lax.broadcasted_iota(I32, (rows, 1), 0), dec_seq.bit_length() - 1)
    r16 = jnp.zeros((rows, r8.shape[1]), F32)
    for h in range(n_heads):
        r16 = jnp.where(row_head == h, r8[h:h + 1, :], r16)
    kt = jnp.concatenate([r[0, 0] for r in kp], axis=1).astype(BF16)
    vt = jnp.concatenate([r[0, 0] for r in vp], axis=1).astype(BF16)
    update(_dot(q, kt) + r16 + cq_ref[...], vt)

    @pl.when(c == pl.num_programs(1) - 1)
    def _():
        o_ref[0] = a_ref[...] / l_ref[...]


def _decode_kernel(pt_ref, *refs, pages, n_da, n_fox, lam_init, dec_seq):
    n_d, n_f = 1 + 2 * pages + 6, 1 + 3 * pages + 3
    d_in, f_in = refs[:n_d], refs[n_d:n_d + n_f]
    od_ref, of_ref = refs[n_d + n_f:n_d + n_f + 2]
    sc = refs[n_d + n_f + 2:]
    _diff_decode_kernel(pt_ref, *d_in, od_ref, *sc[0:3], pages=pages, n_heads=n_da, lam_init=lam_init, dec_seq=dec_seq)
    _fox_decode_kernel(pt_ref, *f_in, of_ref, *sc[3:8], pages=pages, n_heads=n_fox, dec_seq=dec_seq)


def _decode(pt_flat, layer, qbd, dkt, dv4, bias_past, knew, vnew, bias_new, lam_p, subln,
            qfbd, fkt, fvt, flt, kfnew, vfnew_t, lnew, *, pages, n_pages, lam_init, dec_seq):
    DB, drows, dq = qbd.shape
    _, frows, dfx = qfbd.shape
    nc = n_pages // pages
    per_seq = lambda a: pl.BlockSpec((1,) + a.shape[1:], lambda b, c, pt: (b, 0, 0))
    shared = lambda a: pl.BlockSpec(a.shape, lambda b, c, pt: (0, 0))

    def page_spec(jj, r, newest_first):
        chunk = (lambda c: nc - 1 - c) if newest_first else (lambda c: c)
        return pl.BlockSpec((1, 1, r, LANES),
                            lambda b, c, pt: (layer, pt[b * n_pages + chunk(c) * pages + jj], 0, 0))

    in_specs = ([per_seq(qbd)]
                + [page_spec(jj, dq, False) for jj in range(pages)] * 2
                + [pl.BlockSpec((drows, pages * LANES), lambda b, c, pt: (0, c)),
                   per_seq(knew), per_seq(vnew), shared(bias_new), shared(lam_p), shared(subln)]
                + [per_seq(qfbd)]
                + [page_spec(jj, dfx, True) for jj in range(pages)] * 2
                + [page_spec(jj, SUBLANES, True) for jj in range(pages)]
                + [per_seq(kfnew), per_seq(vfnew_t), per_seq(lnew)])
    kern = functools.partial(_decode_kernel, pages=pages, n_da=dq // LANES, n_fox=frows // dec_seq,
                             lam_init=lam_init, dec_seq=dec_seq)
    return pl.pallas_call(
        kern,
        grid_spec=pltpu.PrefetchScalarGridSpec(
            num_scalar_prefetch=1, grid=(DB, nc), in_specs=in_specs,
            out_specs=[pl.BlockSpec((1, drows, LANES), lambda b, c, pt: (b, 0, 0)),
                       pl.BlockSpec((1, frows, dfx), lambda b, c, pt: (b, 0, 0))],
            scratch_shapes=[pltpu.VMEM((drows, 1), F32), pltpu.VMEM((drows, 1), F32), pltpu.VMEM((drows, LANES), F32),
                            pltpu.VMEM((frows, 1), F32), pltpu.VMEM((frows, 1), F32), pltpu.VMEM((frows, dfx), F32),
                            pltpu.VMEM((frows, 1), F32), pltpu.VMEM((SUBLANES, LANES), F32)]),
        out_shape=[jax.ShapeDtypeStruct((DB, drows, LANES), F32), jax.ShapeDtypeStruct((DB, frows, dfx), F32)],
        compiler_params=_cparams(2), name="attn_decode",
    )(pt_flat, qbd, *([dkt] * pages), *([dv4] * pages), bias_past, knew, vnew, bias_new, lam_p, subln,
      qfbd, *([fkt] * pages), *([fvt] * pages), *([flt] * pages), kfnew, vfnew_t, lnew)


def _pad_rows(a, rows):
    return jnp.pad(a, ((0, rows - a.shape[0]),) + ((0, 0),) * (a.ndim - 1))


def _tile(n, pref):
    return pref if n % pref == 0 else n


def kernel(x_prompt, x_sample, cache_dk, cache_dv, cache_fk, cache_fv, cache_flogf, state_conv, state_ffn, page_table, rel_bias, norm1, w_in, b_f, conv_w, diff_lambda, subln, w_out, norm2, w_up, ffn_conv, w_down, norm_f):
    B, S, D = x_prompt.shape
    DB, T_NEW, _ = x_sample.shape
    depth, n_phys, page, n_da, _, da_qk = cache_dk.shape
    da_v = cache_dv.shape[-1]
    n_fox, fox_hd = cache_fk.shape[3], cache_fk.shape[4]
    cw = conv_w.shape[-1]
    dff = w_down.shape[1]
    n_pages = page_table.shape[1]
    past = n_pages * page
    dq, dfx = n_da * 2 * da_qk, n_fox * fox_hd
    assert page == LANES and 2 * da_qk == LANES and da_v == LANES and 2 * fox_hd == LANES
    assert n_fox % 2 == 0 and T_NEW & (T_NEW - 1) == 0 and T_NEW <= SUBLANES // 2
    assert (DB * T_NEW) % LANES == 0 and S % LANES == 0
    da_scale, fox_scale = da_qk ** -0.5, fox_hd ** -0.5
    t = _tile(S, 512)
    rs = DB * T_NEW
    pages = _tile(n_pages, 16)
    new_pad = 2 * SUBLANES
    ffn_chunk = _tile(dff, 256)

    dkt = jnp.transpose(cache_dk, (0, 1, 3, 4, 5, 2)).reshape(depth, n_phys, dq, page)
    dv4 = cache_dv.reshape(depth, n_phys, page * n_da, da_v)
    fkt = jnp.transpose(cache_fk, (0, 1, 3, 4, 2)).reshape(depth, n_phys, dfx, page)
    fvt = jnp.transpose(cache_fv, (0, 1, 3, 4, 2)).reshape(depth, n_phys, dfx, page)
    flt = jnp.pad(jnp.swapaxes(cache_flogf, 2, 3), ((0, 0), (0, 0), (0, SUBLANES - n_fox), (0, 0)))
    pt_flat = page_table.reshape(-1)

    dtiles, bias_past, bias_new = _bias_tiles(rel_bias, t=t, past=past, new_pad=new_pad, dec_seq=T_NEW)

    xp = x_prompt
    xs = x_sample.reshape(1, rs, D)
    zeros_c = jnp.zeros((B, SUBLANES, cw), F32)
    zeros_f = jnp.zeros((B, SUBLANES, 2 * dff), F32)
    eye_da = jnp.eye(n_da, dtype=BF16)
    pair_of_head = (jnp.arange(n_fox)[:, None] // 2 == jnp.arange(n_fox // 2)[None, :]).astype(BF16)
    st_p, st_s = [], []
    for l in range(depth):
        lam_init = 0.8 - 0.6 * math.exp(-0.3 * l)
        final = l == depth - 1
        wl = w_in[l]
        s0, s1 = 3 * cw, 3 * cw + dq
        wn = jnp.concatenate([wl[:, :s1], wl[:, s1 + dq:s1 + 2 * dq], wl[:, s1 + 2 * dq:s1 + 2 * dq + dfx]],
                             axis=1).astype(BF16)
        wt = jnp.concatenate([wl[:, s1:s1 + dq], wl[:, s1 + 2 * dq + dfx:s1 + 2 * dq + 3 * dfx]], axis=1).T.astype(BF16)
        wfl = _pad_rows(wl[:, s1 + 2 * dq + 3 * dfx:].T, 2 * SUBLANES).astype(BF16)
        bft = _pad_rows(b_f[l][:, None], SUBLANES)
        cwp = _pad_rows(conv_w[l], SUBLANES)
        fwp = _pad_rows(ffn_conv[l], SUBLANES)
        g1, g2, gf, gs = norm1[l][None], norm2[l][None], norm_f[None], subln[l][None]
        wo, wu, wd = w_out[l].astype(BF16), w_up[l].astype(BF16), w_down[l].astype(BF16)
        lam_p = diff_lambda[l]
        inproj = functools.partial(_inproj, da_scale=da_scale * LOG2E, fox_scale=fox_scale * LOG2E)

        (conv, cst, qz, v, vb, qfz, kt, ktb, kft, kftb, vft, vftb, lft, ct, ccol) = inproj(
            xp, g1, wn, wt, wfl, bft, cwp, zeros_c, tm=t, period=t, prompt=True)
        od = _diff_prompt(qz, ktb, vb, dtiles, lam_p, gs, t=t, lam_init=lam_init)
        of = _fox_prompt(qfz, kftb, vftb, ct, ccol, t=t)
        xp, fst = _outffn(xp, conv, od, of, wo, g2, wu, fwp, wd, gf, zeros_f,
                          tm=t, period=t, prompt=True, final=final, chunk_w=ffn_chunk)
        st_p.append((
            jnp.transpose(kt.reshape(B, n_da, 2, da_qk, S), (0, 4, 1, 2, 3)),
            v.reshape(B, S, n_da, da_v),
            jnp.transpose(kft.reshape(B, n_fox, fox_hd, S), (0, 3, 1, 2)),
            jnp.transpose(vft.reshape(B, n_fox, fox_hd, S), (0, 3, 1, 2)),
            jnp.swapaxes(lft[:, :n_fox, :], 1, 2),
            cst[:, SUBLANES - 2:, :], fst[:, SUBLANES - 2:, :]))

        pc = (jnp.repeat(state_conv[l][:, 0], T_NEW, axis=0)[None], jnp.repeat(state_conv[l][:, 1], T_NEW, axis=0)[None])
        pf = (jnp.repeat(state_ffn[l][:, 0], T_NEW, axis=0)[None], jnp.repeat(state_ffn[l][:, 1], T_NEW, axis=0)[None])
        (conv_s, gcu_s, qz_s, v_s, vb_s, qfz_s, kt_s, ktb_s, kft_s, kftb_s, vft_s, vftb_s, lft_s) = inproj(
            xs, g1, wn, wt, wfl, bft, cwp, pc, tm=rs, period=T_NEW, prompt=False)
        k_s = kt_s[0].T.reshape(DB, T_NEW, dq)
        kf_s = kft_s[0].T.reshape(DB, T_NEW, dfx)
        vf_s = vft_s[0].T.reshape(DB, T_NEW, dfx)
        lf_s = lft_s[0, :n_fox].T.reshape(DB, T_NEW, n_fox)
        pad_new = lambda a: jnp.pad(a, ((0, 0), (0, new_pad - T_NEW), (0, 0)))
        pad_lane = lambda a: jnp.pad(a, ((0, 0), (0, LANES - T_NEW), (0, 0)))
        q5 = jnp.transpose(qz_s.reshape(DB, T_NEW, n_da, 2, LANES), (0, 2, 3, 1, 4))
        qbd = (q5[:, :, :, :, None, :] * eye_da[None, :, None, None, :, None]).reshape(DB, n_da * 2 * T_NEW, dq)
        qf4 = jnp.transpose(qfz_s.reshape(DB, T_NEW, n_fox, LANES), (0, 2, 1, 3))
        qfbd = (qf4[:, :, :, None, :] * pair_of_head[None, :, None, :, None]).reshape(DB, n_fox * T_NEW, dfx)
        lnew = jnp.pad(jnp.broadcast_to(jnp.swapaxes(lf_s, 1, 2)[:, :, None, :], (DB, n_fox, T_NEW, T_NEW)),
                       ((0, 0), (0, 0), (0, 0), (0, LANES - T_NEW))).reshape(DB, n_fox * T_NEW, LANES)
        od_s, of_s = _decode(pt_flat, l, qbd, dkt, dv4, bias_past, pad_new(k_s).astype(BF16),
                             pad_new(vb_s.reshape(DB, T_NEW, dq)), bias_new, lam_p, gs,
                             qfbd, fkt, fvt, flt, pad_lane(kf_s).astype(BF16),
                             jnp.swapaxes(pad_lane(vf_s), 1, 2).astype(BF16), lnew,
                             pages=pages, n_pages=n_pages, lam_init=lam_init, dec_seq=T_NEW)
        od_s = jnp.transpose(od_s.reshape(DB, n_da, 2, T_NEW, da_v)[:, :, 0], (0, 2, 1, 3)).reshape(1, rs, dq)
        of5 = of_s.reshape(DB, n_fox, T_NEW, n_fox, fox_hd)
        of_s = jnp.transpose(jnp.einsum('bhthd->bhtd', of5), (0, 2, 1, 3)).reshape(1, rs, dfx)
        xs, up_s = _outffn(xs, conv_s, od_s.astype(BF16), of_s.astype(BF16), wo, g2, wu, fwp, wd, gf, pf,
                           tm=rs, period=T_NEW, prompt=False, final=final, chunk_w=ffn_chunk)
        st_s.append((
            k_s.reshape(DB, T_NEW, n_da, 2, da_qk), v_s.reshape(DB, T_NEW, n_da, da_v),
            kf_s.reshape(DB, T_NEW, n_fox, fox_hd), vf_s.reshape(DB, T_NEW, n_fox, fox_hd), lf_s,
            gcu_s.reshape(DB, T_NEW, cw)[:, T_NEW - 2:], up_s.reshape(DB, T_NEW, 2 * dff)[:, T_NEW - 2:]))

    outs = [xp, xs.reshape(DB, T_NEW, D)]
    for j in range(7):
        outs.append(jnp.stack([s[j] for s in st_p]))
        outs.append(jnp.stack([s[j] for s in st_s]))
    return tuple(outs)
```

```python
import functools
import math

import jax
import jax.numpy as jnp
from jax import lax
from jax.experimental import pallas as pl
from jax.experimental.pallas import tpu as pltpu

F32, BF16, I32 = jnp.float32, jnp.bfloat16, jnp.int32
RMS_EPS = 1e-6
NEG_INF = -1e30
NUM_BUCKETS = 32
MAX_DISTANCE = 128
LANES = 128
SUBLANES = 8
VMEM_LIMIT_BYTES = 56 * 2**20
NT_DIMS = (((1,), (1,)), ((), ()))
LOG2E = math.log2(math.e)


def _cparams(n_axes):
    return pltpu.CompilerParams(dimension_semantics=("arbitrary",) * n_axes, vmem_limit_bytes=VMEM_LIMIT_BYTES)


def _dot(a, b):
    return jnp.dot(a, b, preferred_element_type=F32)


def _dot_nt(a, b):
    return lax.dot_general(a, b, NT_DIMS, preferred_element_type=F32)


def _rms(x, g):
    return x * lax.rsqrt(jnp.mean(x * x, axis=-1, keepdims=True) + RMS_EPS) * g


def _log_sigmoid(x):
    return jnp.minimum(x, 0.0) - jnp.log1p(jnp.exp(-jnp.abs(x)))


def _conv3(u, p0, p1, w_ref, cols, period):
    rows = u.shape[0]
    t = lax.broadcasted_iota(I32, (rows, 1), 0)
    if period < rows:
        t = jnp.bitwise_and(t, period - 1)
    sh1 = jnp.where(t == 0, p1, pltpu.roll(u, 1, 0))
    sh2 = jnp.where(t == 0, p0, jnp.where(t == 1, p1, pltpu.roll(u, 2, 0)))
    return sh2 * w_ref[0:1, cols] + sh1 * w_ref[1:2, cols] + u * w_ref[2:3, cols]


def _prefix_sum_lanes(y, lane):
    for s in (1, 2, 4, 8, 16, 32, 64):
        y = y + jnp.where(lane >= s, pltpu.roll(y, s, 1), 0.0)
    return y


def _inproj_kernel(*refs, tm, period, prompt, da_scale, fox_scale, n_da, n_pairs):
    if prompt:
        (x_ref, g_ref, wn_ref, wt_ref, wfl_ref, bf_ref, cw_ref, pref_ref,
         conv_ref, st_ref, qz_ref, v_ref, vb_ref, qfz_ref, kt_ref, ktb_ref, kft_ref, kftb_ref,
         vft_ref, vftb_ref, lft_ref, ct_ref, ccol_ref, carry_ref, ccarry_ref) = refs
    else:
        (x_ref, g_ref, wn_ref, wt_ref, wfl_ref, bf_ref, cw_ref, p0_ref, p1_ref,
         conv_ref, st_ref, qz_ref, v_ref, vb_ref, qfz_ref, kt_ref, ktb_ref, kft_ref, kftb_ref,
         vft_ref, vftb_ref, lft_ref) = refs
    i = pl.program_id(1)
    cw = cw_ref.shape[1]
    dq = n_da * LANES
    dfx = n_pairs * LANES
    h = _rms(x_ref[0], g_ref[...]).astype(BF16)

    ugc = _dot(h, wn_ref[:, 0:3 * cw])
    gcu = ugc[:, 2 * cw:3 * cw] * ugc[:, 0:cw]
    if prompt:
        @pl.when(i == 0)
        def _():
            carry_ref[...] = pref_ref[0]
            ccarry_ref[...] = jnp.zeros_like(ccarry_ref)
        p0, p1 = carry_ref[6:7, :], carry_ref[7:8, :]
    else:
        p0, p1 = p0_ref[0], p1_ref[0]
    cy = _conv3(gcu, p0, p1, cw_ref, slice(None), period)
    conv_ref[0] = (ugc[:, cw:2 * cw] * cy).astype(BF16)
    if prompt:
        carry_ref[...] = gcu[tm - SUBLANES:tm]
        st_ref[0] = gcu[tm - SUBLANES:tm]
    else:
        st_ref[0] = gcu

    lo = lax.broadcasted_iota(I32, (tm, LANES), 1) < (LANES // 2)
    c0 = 3 * cw
    q = _dot(h, wn_ref[:, c0:c0 + dq]) * da_scale
    for hh in range(n_da):
        qh = q[:, hh * LANES:(hh + 1) * LANES]
        qz_ref[0, :, 2 * hh * LANES:(2 * hh + 1) * LANES] = jnp.where(lo, qh, 0.0).astype(BF16)
        qz_ref[0, :, (2 * hh + 1) * LANES:(2 * hh + 2) * LANES] = jnp.where(lo, 0.0, qh).astype(BF16)
    c0 += dq
    v = _dot(h, wn_ref[:, c0:c0 + dq])
    v_ref[0] = v
    vb_ref[0] = v.astype(BF16)
    c0 += dq
    qf = _dot(h, wn_ref[:, c0:c0 + dfx]) * fox_scale
    for p in range(n_pairs):
        qp = qf[:, p * LANES:(p + 1) * LANES]
        qfz_ref[0, :, 2 * p * LANES:(2 * p + 1) * LANES] = jnp.where(lo, qp, 0.0).astype(BF16)
        qfz_ref[0, :, (2 * p + 1) * LANES:(2 * p + 2) * LANES] = jnp.where(lo, 0.0, qp).astype(BF16)

    kt = _dot_nt(wt_ref[0:dq, :], h)
    kt_ref[0] = kt
    kft = _dot_nt(wt_ref[dq:dq + dfx, :], h)
    kft_ref[0] = kft
    vft = _dot_nt(wt_ref[dq + dfx:dq + 2 * dfx, :], h)
    vft_ref[0] = vft
    if prompt:
        ktb_ref[0, 0] = kt.astype(BF16)
        kftb_ref[0, 0] = kft.astype(BF16)
        vftb_ref[0, 0] = vft.astype(BF16)
    else:
        ktb_ref[0] = kt.astype(BF16)
        kftb_ref[0] = kft.astype(BF16)
        vftb_ref[0] = vft.astype(BF16)
    lf = _log_sigmoid(_dot_nt(wfl_ref[...], h)[0:SUBLANES] + bf_ref[...])
    lft_ref[0] = lf

    if prompt:
        lane = lax.broadcasted_iota(I32, (SUBLANES, LANES), 1)
        carry = ccarry_ref[...]
        blocks = []
        for blk in range(tm // LANES):
            y = _prefix_sum_lanes(lf[:, blk * LANES:(blk + 1) * LANES], lane) + carry
            carry = jnp.broadcast_to(y[:, LANES - 1:LANES], (SUBLANES, LANES))
            blocks.append(y)
        ccarry_ref[...] = carry
        ct = jnp.concatenate(blocks, axis=1) * LOG2E
        ct_ref[0, 0] = ct
        ccol_ref[0] = jnp.concatenate([ct, jnp.zeros((LANES - SUBLANES, tm), F32)], axis=0).T


def _inproj(x, g1, wn, wt, wfl, bft, cwp, pre, *, tm, period, prompt, da_scale, fox_scale):
    G, R, D = x.shape
    nt = R // tm
    cw = cwp.shape[1]
    dq = (wn.shape[1] - 3 * cw) * 2 // 5
    dfx = dq // 2
    n_da, n_pairs = dq // LANES, dfx // LANES
    row = lambda c: pl.BlockSpec((1, tm, c), lambda g, i: (g, i, 0))
    colT = lambda r: pl.BlockSpec((1, r, tm), lambda g, i: (g, 0, i))
    chunkT = lambda r: pl.BlockSpec((1, 1, r, tm), lambda g, i: (g, i, 0, 0))
    full = lambda a: pl.BlockSpec(a.shape, lambda g, i: (0,) * a.ndim)
    in_specs = [row(D), full(g1), full(wn), full(wt), full(wfl), full(bft), full(cwp)]
    if prompt:
        in_specs += [pl.BlockSpec((1, SUBLANES, cw), lambda g, i: (g, 0, 0))]
        ins = (x, g1, wn, wt, wfl, bft, cwp, pre)
        st_spec, st_shape = pl.BlockSpec((1, SUBLANES, cw), lambda g, i: (g, 0, 0)), (G, SUBLANES, cw)
        tb = lambda r: (chunkT(r), jax.ShapeDtypeStruct((G, nt, r, tm), BF16))
    else:
        in_specs += [row(cw), row(cw)]
        ins = (x, g1, wn, wt, wfl, bft, cwp, pre[0], pre[1])
        st_spec, st_shape = row(cw), (G, R, cw)
        tb = lambda r: (colT(r), jax.ShapeDtypeStruct((G, r, R), BF16))
    outs = [
        (row(cw), jax.ShapeDtypeStruct((G, R, cw), BF16)),
        (st_spec, jax.ShapeDtypeStruct(st_shape, F32)),
        (row(2 * dq), jax.ShapeDtypeStruct((G, R, 2 * dq), BF16)),
        (row(dq), jax.ShapeDtypeStruct((G, R, dq), F32)),
        (row(dq), jax.ShapeDtypeStruct((G, R, dq), BF16)),
        (row(2 * dfx), jax.ShapeDtypeStruct((G, R, 2 * dfx), BF16)),
        (colT(dq), jax.ShapeDtypeStruct((G, dq, R), F32)),
        tb(dq),
        (colT(dfx), jax.ShapeDtypeStruct((G, dfx, R), F32)),
        tb(dfx),
        (colT(dfx), jax.ShapeDtypeStruct((G, dfx, R), F32)),
        tb(dfx),
        (colT(SUBLANES), jax.ShapeDtypeStruct((G, SUBLANES, R), F32)),
    ]
    scratch = []
    if prompt:
        outs += [
            (chunkT(SUBLANES), jax.ShapeDtypeStruct((G, nt, SUBLANES, tm), F32)),
            (row(LANES), jax.ShapeDtypeStruct((G, R, LANES), F32)),
        ]
        scratch = [pltpu.VMEM((SUBLANES, cw), F32), pltpu.VMEM((SUBLANES, LANES), F32)]
    kern = functools.partial(_inproj_kernel, tm=tm, period=period, prompt=prompt, da_scale=da_scale,
                             fox_scale=fox_scale, n_da=n_da, n_pairs=n_pairs)
    return pl.pallas_call(
        kern, grid=(G, nt), in_specs=in_specs, out_specs=[o[0] for o in outs],
        out_shape=[o[1] for o in outs], scratch_shapes=scratch, compiler_params=_cparams(2),
        name="inproj_prompt" if prompt else "inproj_sample")(*ins)


def _bias_of_distance(n, tab_ref, h, n_heads):
    max_exact = NUM_BUCKETS // 2
    nf = jnp.maximum(n, 1).astype(F32)
    large = max_exact + (jnp.log(nf / max_exact) / math.log(MAX_DISTANCE / max_exact)
                         * (NUM_BUCKETS - max_exact)).astype(I32)
    bucket = jnp.where(n < max_exact, n, jnp.minimum(large, NUM_BUCKETS - 1))
    far = tab_ref[(NUM_BUCKETS - 1) * n_heads + h]
    out = jnp.zeros(n.shape, F32)
    for b in range(NUM_BUCKETS - 1):
        out = jnp.where(bucket == b, (tab_ref[b * n_heads + h] - far) * LOG2E, out)
    return out


def _bias_kernel(tab_ref, d_ref, bp_ref, bn_ref, *, t, past, n_heads, new_pad, dec_seq):
    r = lax.broadcasted_iota(I32, (t, t), 0)
    c = lax.broadcasted_iota(I32, (t, t), 1)
    rp = lax.broadcasted_iota(I32, (SUBLANES, past), 0)
    kp = lax.broadcasted_iota(I32, (SUBLANES, past), 1)
    rn = lax.broadcasted_iota(I32, (SUBLANES, new_pad), 0)
    kn = lax.broadcasted_iota(I32, (SUBLANES, new_pad), 1)
    tp = jnp.bitwise_and(rp, dec_seq - 1)
    tn = jnp.bitwise_and(rn, dec_seq - 1)
    for h in range(n_heads):
        d_ref[h, 0] = jnp.where(r >= c, _bias_of_distance(jnp.maximum(r - c, 0), tab_ref, h, n_heads), NEG_INF)
        d_ref[h, 1] = _bias_of_distance(t + r - c, tab_ref, h, n_heads)
        bp_ref[h * SUBLANES:(h + 1) * SUBLANES, :] = _bias_of_distance(past + tp - kp, tab_ref, h, n_heads)
        bn_ref[h * SUBLANES:(h + 1) * SUBLANES, :] = jnp.where(
            kn <= tn, _bias_of_distance(jnp.maximum(tn - kn, 0), tab_ref, h, n_heads), NEG_INF)


def _bias_tiles(rel_bias, *, t, past, new_pad, dec_seq):
    n_heads = rel_bias.shape[1]
    kern = functools.partial(_bias_kernel, t=t, past=past, n_heads=n_heads, new_pad=new_pad, dec_seq=dec_seq)
    return pl.pallas_call(
        kern,
        in_specs=[pl.BlockSpec(memory_space=pltpu.SMEM)],
        out_specs=[pl.BlockSpec(memory_space=pltpu.VMEM)] * 3,
        out_shape=[jax.ShapeDtypeStruct((n_heads, 2, t, t), F32),
                   jax.ShapeDtypeStruct((n_heads * SUBLANES, past), F32),
                   jax.ShapeDtypeStruct((n_heads * SUBLANES, new_pad), F32)],
        compiler_params=pltpu.CompilerParams(vmem_limit_bytes=VMEM_LIMIT_BYTES),
        name="rel_bias_tiles")(rel_bias.reshape(-1))


def _lambda_of(lp, lam_init):
    a = jnp.sum(lp[0:1] * lp[1:2], axis=-1, keepdims=True)
    b = jnp.sum(lp[2:3] * lp[3:4], axis=-1, keepdims=True)
    return jnp.exp(a) - jnp.exp(b) + lam_init


def _flash_update(s, m_ref, acc_ref, pv, row_off=None):
    blocks = [s[:, c * LANES:(c + 1) * LANES] for c in range(s.shape[1] // LANES)]
    rmax = jnp.max(functools.reduce(jnp.maximum, blocks), axis=-1, keepdims=True)
    m_old = m_ref[...]
    m_new = jnp.maximum(m_old, rmax if row_off is None else rmax + row_off)
    alpha = jnp.exp2(m_old - m_new)
    shift = m_new if row_off is None else m_new - row_off
    p = jnp.concatenate([jnp.exp2(b - shift).astype(BF16) for b in blocks], axis=1)
    acc_ref[...] = jnp.concatenate([alpha, alpha], axis=1) * acc_ref[...] + pv(p)
    m_ref[...] = m_new


CHUNKS_PER_BLOCK = 4


def _run_chunks(n, chunk, last):
    def group(jg, carry):
        for u in range(CHUNKS_PER_BLOCK):
            chunk(CHUNKS_PER_BLOCK * jg + u)
        return carry

    n_groups = n // CHUNKS_PER_BLOCK
    lax.fori_loop(0, n_groups, group, 0)
    base = n_groups * CHUNKS_PER_BLOCK
    for rem in range(CHUNKS_PER_BLOCK):
        @pl.when(n - base == rem)
        def _():
            for u in range(rem):
                chunk(base + u)
            last()


def _diff_prompt_kernel(q_ref, k_ref, v_ref, d_ref, lam_ref, g_ref, o_ref, m1, a1, m2, a2, *, t, lam_init):
    i = pl.program_id(2)
    for m, a in ((m1, a1), (m2, a2)):
        m[...] = jnp.full_like(m, NEG_INF)
        a[...] = jnp.zeros_like(a)
    q1, q2 = q_ref[0, :, 0:LANES], q_ref[0, :, LANES:2 * LANES]

    corner = d_ref[0, 1, 0:LANES, t - LANES:t]

    def chunk(j, diagonal=False):
        kt = k_ref[0, j]
        v = v_ref[0, pl.ds(pl.multiple_of(j * t, t), t), :]
        v1 = jnp.concatenate([v, jnp.ones_like(v)], axis=1)
        for q, m, a in ((q1, m1, a1), (q2, m2, a2)):
            s = _dot(q, kt)
            if diagonal:
                s = s + d_ref[0, 0]
            else:
                near = s[0:LANES, t - LANES:] + jnp.where(j == i - 1, corner, 0.0)
                if t > LANES:
                    top = jnp.concatenate([s[0:LANES, :t - LANES], near], axis=1)
                    s = jnp.concatenate([top, s[LANES:]], axis=0)
                else:
                    s = near
            _flash_update(s, m, a, lambda p: _dot(p, v1))

    _run_chunks(i, chunk, lambda: chunk(i, diagonal=True))

    lam = _lambda_of(lam_ref[...], lam_init)
    o = a1[:, 0:LANES] / a1[:, LANES:] - lam * (a2[:, 0:LANES] / a2[:, LANES:])
    o_ref[0] = (_rms(o, g_ref[...]) * (1.0 - lam_init)).astype(BF16)


def _diff_prompt(qz, ktb, vb, dtiles, lam_p, subln, *, t, lam_init):
    B, S, _ = qz.shape
    H = dtiles.shape[0]
    nq = S // t
    kern = functools.partial(_diff_prompt_kernel, t=t, lam_init=lam_init)
    return pl.pallas_call(
        kern, grid=(B, H, nq),
        in_specs=[pl.BlockSpec((1, t, 2 * LANES), lambda b, h, i: (b, i, h)),
                  pl.BlockSpec((1, nq, LANES, t), lambda b, h, i: (b, 0, h, 0)),
                  pl.BlockSpec((1, S, LANES), lambda b, h, i: (b, 0, h)),
                  pl.BlockSpec((1, 2, t, t), lambda b, h, i: (h, 0, 0, 0)),
                  pl.BlockSpec(lam_p.shape, lambda b, h, i: (0, 0)),
                  pl.BlockSpec(subln.shape, lambda b, h, i: (0, 0))],
        out_specs=pl.BlockSpec((1, t, LANES), lambda b, h, i: (b, i, h)),
        out_shape=jax.ShapeDtypeStruct((B, S, H * LANES), BF16),
        scratch_shapes=[pltpu.VMEM((t, LANES), F32), pltpu.VMEM((t, 2 * LANES), F32)] * 2,
        compiler_params=_cparams(3), name="diff_attn_prompt")(qz, ktb, vb, dtiles, lam_p, subln)


def _fox_prompt_kernel(q_ref, k_ref, v_ref, c_ref, cc_ref, o_ref, m0, a0, m1, a1, *, t):
    pr = pl.program_id(1)
    i = pl.program_id(2)
    state = ((m0, a0), (m1, a1))
    for m, a in state:
        m[...] = jnp.full_like(m, NEG_INF)
        a[...] = jnp.zeros_like(a)
    lane = lax.broadcasted_iota(I32, (t, LANES), 1)
    ccol = cc_ref[0]
    heads = []
    for e in range(2):
        hsel = lane == (2 * pr + e)
        cq = jnp.sum(jnp.where(hsel, ccol, 0.0), axis=-1, keepdims=True)
        cbase = cq[0:1, :]
        heads.append((q_ref[0, :, e * LANES:(e + 1) * LANES], jnp.broadcast_to(cq - cbase, (t, LANES)), cbase, state[e]))

    def chunk(j, diag):
        kt = k_ref[0, j]
        vt = v_ref[0, j]
        vt1 = jnp.concatenate([vt, jnp.ones_like(vt)], axis=0)
        crow = c_ref[0, j]
        crow_idx = lax.broadcasted_iota(I32, crow.shape, 0)
        for e, (q, cqr, cbase, (m, a)) in enumerate(heads):
            ck = jnp.sum(jnp.where(crow_idx == 2 * pr + e, crow, 0.0), axis=0, keepdims=True)
            z = _dot(q, kt) - (ck - cbase)
            if diag:
                r = lax.broadcasted_iota(I32, (t, t), 0)
                c = lax.broadcasted_iota(I32, (t, t), 1)
                z = jnp.where(r >= c, z, NEG_INF)
            _flash_update(z, m, a, lambda p: _dot_nt(p, vt1), row_off=cqr)

    _run_chunks(i, lambda j: chunk(j, False), lambda: chunk(i, True))
    o_ref[0] = jnp.where(lane < LANES // 2, a0[:, 0:LANES] / a0[:, LANES:], a1[:, 0:LANES] / a1[:, LANES:]).astype(BF16)


def _fox_prompt(qfz, kftb, vftb, ct, ccol, *, t):
    B, S, _ = qfz.shape
    nq = S // t
    n_pairs = kftb.shape[2] // LANES
    kern = functools.partial(_fox_prompt_kernel, t=t)
    return pl.pallas_call(
        kern, grid=(B, n_pairs, nq),
        in_specs=[pl.BlockSpec((1, t, 2 * LANES), lambda b, p, i: (b, i, p)),
                  pl.BlockSpec((1, nq, LANES, t), lambda b, p, i: (b, 0, p, 0)),
                  pl.BlockSpec((1, nq, LANES, t), lambda b, p, i: (b, 0, p, 0)),
                  pl.BlockSpec((1, nq, SUBLANES, t), lambda b, p, i: (b, 0, 0, 0)),
                  pl.BlockSpec((1, t, LANES), lambda b, p, i: (b, i, 0))],
        out_specs=pl.BlockSpec((1, t, LANES), lambda b, p, i: (b, i, p)),
        out_shape=jax.ShapeDtypeStruct((B, S, n_pairs * LANES), BF16),
        scratch_shapes=[pltpu.VMEM((t, LANES), F32), pltpu.VMEM((t, 2 * LANES), F32)] * 2,
        compiler_params=_cparams(3), name="fox_attn_prompt")(qfz, kftb, vftb, ct, ccol)


def _outffn_kernel(*refs, tm, period, prompt, final, chunk_w):
    if prompt:
        (x_ref, cv_ref, od_ref, of_ref, wo_ref, g2_ref, wu_ref, fw_ref, wd_ref, gf_ref, pref_ref,
         xo_ref, st_ref, act_ref, carry_ref) = refs
    else:
        (x_ref, cv_ref, od_ref, of_ref, wo_ref, g2_ref, wu_ref, fw_ref, wd_ref, gf_ref, p0_ref, p1_ref,
         xo_ref, st_ref, act_ref) = refs
    i = pl.program_id(1)
    cw, dw = cv_ref.shape[2], od_ref.shape[2]
    dff = wd_ref.shape[0]
    x1 = (x_ref[0] + _dot(cv_ref[0], wo_ref[0:cw, :]) + _dot(od_ref[0], wo_ref[cw:cw + dw, :])
          + _dot(of_ref[0], wo_ref[cw + dw:, :]))
    xn = _rms(x1, g2_ref[...]).astype(BF16)
    if prompt:
        @pl.when(i == 0)
        def _():
            carry_ref[...] = pref_ref[0]
    for c in range(dff // chunk_w):
        halves = []
        for base in (c * chunk_w, dff + c * chunk_w):
            cols = slice(base, base + chunk_w)
            up = _dot(xn, wu_ref[:, cols])
            if prompt:
                p0, p1 = carry_ref[6:7, cols], carry_ref[7:8, cols]
            else:
                p0, p1 = p0_ref[0, :, cols], p1_ref[0, :, cols]
            halves.append(_conv3(up, p0, p1, fw_ref, cols, period))
            if prompt:
                carry_ref[:, cols] = up[tm - SUBLANES:tm]
                st_ref[0, :, cols] = up[tm - SUBLANES:tm]
            else:
                st_ref[0, :, cols] = up
        val, gate = halves
        act = gate * (1.0 / (1.0 + jnp.exp(-gate))) * val
        act_ref[:, c * chunk_w:(c + 1) * chunk_w] = act.astype(BF16)
    x2 = x1 + _dot(act_ref[...], wd_ref[...])
    xo_ref[0] = _rms(x2, gf_ref[...]) if final else x2


def _outffn(x, conv, od, of, wo, g2, wu, fw, wd, gf, pre, *, tm, period, prompt, final, chunk_w):
    G, R, D = x.shape
    nt = R // tm
    dff2 = wu.shape[1]
    row = lambda c: pl.BlockSpec((1, tm, c), lambda g, i: (g, i, 0))
    once = lambda a: pl.BlockSpec(a.shape, lambda g, i: (0,) * a.ndim, pipeline_mode=pl.Buffered(1))
    in_specs = [row(D), row(conv.shape[2]), row(od.shape[2]), row(of.shape[2]),
                once(wo), once(g2), once(wu), once(fw), once(wd), once(gf)]
    if prompt:
        in_specs += [pl.BlockSpec((1, SUBLANES, dff2), lambda g, i: (g, 0, 0))]
        ins = (x, conv, od, of, wo, g2, wu, fw, wd, gf, pre)
        st_spec, st_shape = pl.BlockSpec((1, SUBLANES, dff2), lambda g, i: (g, 0, 0)), (G, SUBLANES, dff2)
        scratch = [pltpu.VMEM((tm, dff2 // 2), BF16), pltpu.VMEM((SUBLANES, dff2), F32)]
    else:
        in_specs += [row(dff2), row(dff2)]
        ins = (x, conv, od, of, wo, g2, wu, fw, wd, gf, pre[0], pre[1])
        st_spec, st_shape = row(dff2), (G, R, dff2)
        scratch = [pltpu.VMEM((tm, dff2 // 2), BF16)]
    kern = functools.partial(_outffn_kernel, tm=tm, period=period, prompt=prompt, final=final, chunk_w=chunk_w)
    return pl.pallas_call(
        kern, grid=(G, nt), in_specs=in_specs, out_specs=[row(D), st_spec],
        out_shape=[jax.ShapeDtypeStruct((G, R, D), F32), jax.ShapeDtypeStruct(st_shape, F32)],
        scratch_shapes=scratch, compiler_params=_cparams(2),
        name="outffn_prompt" if prompt else "outffn_sample")(*ins)


def _diff_decode_kernel(pt_ref, q_ref, *refs, pages, n_heads, lam_init, dec_seq):
    kp = refs[0:pages]
    vp = refs[pages:2 * pages]
    bp_ref, kn_ref, vn_ref, bn_ref, lam_ref, g_ref, o_ref, m_ref, l_ref, a_ref = refs[2 * pages:]
    c = pl.program_id(1)
    q = q_ref[0]
    rows = q.shape[0]

    def update(s, pv):
        m_old = m_ref[...]
        m_new = jnp.maximum(m_old, jnp.max(s, axis=-1, keepdims=True))
        alpha = jnp.exp2(m_old - m_new)
        p = jnp.exp2(s - m_new)
        l_ref[...] = alpha * l_ref[...] + jnp.sum(p, axis=-1, keepdims=True)
        a_ref[...] = alpha * a_ref[...] + pv(p.astype(BF16))
        m_ref[...] = m_new

    def per_head_pv(p, v_of_head):
        outs = []
        for h in range(n_heads):
            g = (h // 2) * 2 * SUBLANES
            res = _dot(p[g:g + 2 * SUBLANES, :], v_of_head(h))
            outs.append(res[(h % 2) * SUBLANES:(h % 2 + 1) * SUBLANES, :])
        return jnp.concatenate(outs, axis=0)

    @pl.when(c == 0)
    def _():
        m_ref[...] = jnp.full_like(m_ref, NEG_INF)
        l_ref[...] = jnp.zeros_like(l_ref)
        a_ref[...] = jnp.zeros_like(a_ref)
        vn = vn_ref[0]
        update(_dot_nt(q, kn_ref[0]) + bn_ref[...],
               lambda p: per_head_pv(p, lambda h: vn[:, h * LANES:(h + 1) * LANES]))

    kt = jnp.concatenate([r[0, 0] for r in kp], axis=1).astype(BF16)
    s = _dot(q, kt) + bp_ref[...]

    def v_of_head(h):
        return jnp.concatenate([r[0, 0, pl.ds(h, LANES, stride=n_heads), :] for r in vp], axis=0).astype(BF16)

    update(s, lambda p: per_head_pv(p, v_of_head))

    @pl.when(c == pl.num_programs(1) - 1)
    def _():
        lam = _lambda_of(lam_ref[...], lam_init)
        on = a_ref[...] / l_ref[...]
        o = on - lam * pltpu.roll(on, rows - dec_seq, 0)
        o_ref[0] = _rms(o, g_ref[...]) * (1.0 - lam_init)


def _fox_decode_kernel(pt_ref, q_ref, *refs, pages, n_heads, dec_seq):
    kp = refs[0:pages]
    vp = refs[pages:2 * pages]
    fp = refs[2 * pages:3 * pages]
    kn_ref, vn_ref, ln_ref, o_ref, m_ref, l_ref, a_ref, cq_ref, run_ref = refs[3 * pages:]
    c = pl.program_id(1)
    q = q_ref[0]
    rows = q.shape[0]

    def update(z, vt):
        m_old = m_ref[...]
        m_new = jnp.maximum(m_old, jnp.max(z, axis=-1, keepdims=True))
        alpha = jnp.exp2(m_old - m_new)
        p = jnp.exp2(z - m_new)
        l_ref[...] = alpha * l_ref[...] + jnp.sum(p, axis=-1, keepdims=True)
        a_ref[...] = alpha * a_ref[...] + _dot_nt(p.astype(BF16), vt)
        m_ref[...] = m_new

    @pl.when(c == 0)
    def _():
        m_ref[...] = jnp.full_like(m_ref, NEG_INF)
        l_ref[...] = jnp.zeros_like(l_ref)
        a_ref[...] = jnp.zeros_like(a_ref)
        run_ref[...] = jnp.zeros_like(run_ref)
        lane = lax.broadcasted_iota(I32, (rows, LANES), 1)
        trow = jnp.bitwise_and(lax.broadcasted_iota(I32, (rows, LANES), 0), dec_seq - 1)
        cnew = ln_ref[0] * LOG2E
        for s_ in (1, 2):
            cnew = cnew + jnp.where(lane >= s_, pltpu.roll(cnew, s_, 1), 0.0)
        cq = jnp.sum(jnp.where(lane == trow, cnew, 0.0), axis=-1, keepdims=True)
        cq_ref[...] = cq
        z = jnp.where(lane <= trow, _dot_nt(q, kn_ref[0]) + cq - cnew, NEG_INF)
        update(z, vn_ref[0])

    lane8 = lax.broadcasted_iota(I32, (SUBLANES, LANES), 1)
    later = run_ref[...]
    r_pages = [None] * pages
    for jj in reversed(range(pages)):
        y = _prefix_sum_lanes(fp[jj][0, 0] * LOG2E, lane8)
        tot = jnp.broadcast_to(y[:, LANES - 1:LANES], (SUBLANES, LANES))
        r_pages[jj] = tot - y + later
        later = later + tot
    run_ref[...] = later
    r8 = jnp.concatenate(r_pages, axis=1)
    row_head = lax.shift_right_logical(lax.broadcasted_iota(I32, (rows, 1), 0), dec_seq.bit_length() - 1)
    r16 = jnp.zeros((rows, r8.shape[1]), F32)
    for h in range(n_heads):
        r16 = jnp.where(row_head == h, r8[h:h + 1, :], r16)
    kt = jnp.concatenate([r[0, 0] for r in kp], axis=1).astype(BF16)
    vt = jnp.concatenate([r[0, 0] for r in vp], axis=1).astype(BF16)
    update(_dot(q, kt) + r16 + cq_ref[...], vt)

    @pl.when(c == pl.num_programs(1) - 1)
    def _():
        o_ref[0] = a_ref[...] / l_ref[...]


def _decode_kernel(pt_ref, *refs, pages, n_da, n_fox, lam_init, dec_seq):
    n_d, n_f = 1 + 2 * pages + 6, 1 + 3 * pages + 3
    d_in, f_in = refs[:n_d], refs[n_d:n_d + n_f]
    od_ref, of_ref = refs[n_d + n_f:n_d + n_f + 2]
    sc = refs[n_d + n_f + 2:]
    _diff_decode_kernel(pt_ref, *d_in, od_ref, *sc[0:3], pages=pages, n_heads=n_da, lam_init=lam_init, dec_seq=dec_seq)
    _fox_decode_kernel(pt_ref, *f_in, of_ref, *sc[3:8], pages=pages, n_heads=n_fox, dec_seq=dec_seq)


def _decode(pt_flat, layer, qbd, dkt, dv4, bias_past, knew, vnew, bias_new, lam_p, subln,
            qfbd, fkt, fvt, flt, kfnew, vfnew_t, lnew, *, pages, n_pages, lam_init, dec_seq):
    DB, drows, dq = qbd.shape
    _, frows, dfx = qfbd.shape
    nc = n_pages // pages
    per_seq = lambda a: pl.BlockSpec((1,) + a.shape[1:], lambda b, c, pt: (b, 0, 0))
    shared = lambda a: pl.BlockSpec(a.shape, lambda b, c, pt: (0, 0))

    def page_spec(jj, r, newest_first):
        chunk = (lambda c: nc - 1 - c) if newest_first else (lambda c: c)
        return pl.BlockSpec((1, 1, r, LANES),
                            lambda b, c, pt: (layer, pt[b * n_pages + chunk(c) * pages + jj], 0, 0))

    in_specs = ([per_seq(qbd)]
                + [page_spec(jj, dq, False) for jj in range(pages)] * 2
                + [pl.BlockSpec((drows, pages * LANES), lambda b, c, pt: (0, c)),
                   per_seq(knew), per_seq(vnew), shared(bias_new), shared(lam_p), shared(subln)]
                + [per_seq(qfbd)]
                + [page_spec(jj, dfx, True) for jj in range(pages)] * 2
                + [page_spec(jj, SUBLANES, True) for jj in range(pages)]
                + [per_seq(kfnew), per_seq(vfnew_t), per_seq(lnew)])
    kern = functools.partial(_decode_kernel, pages=pages, n_da=dq // LANES, n_fox=frows // dec_seq,
                             lam_init=lam_init, dec_seq=dec_seq)
    return pl.pallas_call(
        kern,
        grid_spec=pltpu.PrefetchScalarGridSpec(
            num_scalar_prefetch=1, grid=(DB, nc), in_specs=in_specs,
            out_specs=[pl.BlockSpec((1, drows, LANES), lambda b, c, pt: (b, 0, 0)),
                       pl.BlockSpec((1, frows, dfx), lambda b, c, pt: (b, 0, 0))],
            scratch_shapes=[pltpu.VMEM((drows, 1), F32), pltpu.VMEM((drows, 1), F32), pltpu.VMEM((drows, LANES), F32),
                            pltpu.VMEM((frows, 1), F32), pltpu.VMEM((frows, 1), F32), pltpu.VMEM((frows, dfx), F32),
                            pltpu.VMEM((frows, 1), F32), pltpu.VMEM((SUBLANES, LANES), F32)]),
        out_shape=[jax.ShapeDtypeStruct((DB, drows, LANES), F32), jax.ShapeDtypeStruct((DB, frows, dfx), F32)],
        compiler_params=_cparams(2), name="attn_decode",
    )(pt_flat, qbd, *([dkt] * pages), *([dv4] * pages), bias_past, knew, vnew, bias_new, lam_p, subln,
      qfbd, *([fkt] * pages), *([fvt] * pages), *([flt] * pages), kfnew, vfnew_t, lnew)


def _pad_rows(a, rows):
    return jnp.pad(a, ((0, rows - a.shape[0]),) + ((0, 0),) * (a.ndim - 1))


def _tile(n, pref):
    return pref if n % pref == 0 else n


def kernel(x_prompt, x_sample, cache_dk, cache_dv, cache_fk, cache_fv, cache_flogf, state_conv, state_ffn, page_table, rel_bias, norm1, w_in, b_f, conv_w, diff_lambda, subln, w_out, norm2, w_up, ffn_conv, w_down, norm_f):
    B, S, D = x_prompt.shape
    DB, T_NEW, _ = x_sample.shape
    depth, n_phys, page, n_da, _, da_qk = cache_dk.shape
    da_v = cache_dv.shape[-1]
    n_fox, fox_hd = cache_fk.shape[3], cache_fk.shape[4]
    cw = conv_w.shape[-1]
    dff = w_down.shape[1]
    n_pages = page_table.shape[1]
    past = n_pages * page
    dq, dfx = n_da * 2 * da_qk, n_fox * fox_hd
    assert page == LANES and 2 * da_qk == LANES and da_v == LANES and 2 * fox_hd == LANES
    assert n_fox % 2 == 0 and T_NEW & (T_NEW - 1) == 0 and T_NEW <= SUBLANES // 2
    assert (DB * T_NEW) % LANES == 0 and S % LANES == 0
    da_scale, fox_scale = da_qk ** -0.5, fox_hd ** -0.5
    t = _tile(S, 512)
    rs = DB * T_NEW
    pages = _tile(n_pages, 32)
    new_pad = 2 * SUBLANES
    ffn_chunk = _tile(dff, 256)

    dkt = jnp.transpose(cache_dk, (0, 1, 3, 4, 5, 2)).reshape(depth, n_phys, dq, page)
    dv4 = cache_dv.reshape(depth, n_phys, page * n_da, da_v)
    fkt = jnp.transpose(cache_fk, (0, 1, 3, 4, 2)).reshape(depth, n_phys, dfx, page)
    fvt = jnp.transpose(cache_fv, (0, 1, 3, 4, 2)).reshape(depth, n_phys, dfx, page)
    flt = jnp.pad(jnp.swapaxes(cache_flogf, 2, 3), ((0, 0), (0, 0), (0, SUBLANES - n_fox), (0, 0)))
    pt_flat = page_table.reshape(-1)

    dtiles, bias_past, bias_new = _bias_tiles(rel_bias, t=t, past=past, new_pad=new_pad, dec_seq=T_NEW)

    xp = x_prompt
    xs = x_sample.reshape(1, rs, D)
    zeros_c = jnp.zeros((B, SUBLANES, cw), F32)
    zeros_f = jnp.zeros((B, SUBLANES, 2 * dff), F32)
    eye_da = jnp.eye(n_da, dtype=BF16)
    pair_of_head = (jnp.arange(n_fox)[:, None] // 2 == jnp.arange(n_fox // 2)[None, :]).astype(BF16)
    st_p, st_s = [], []
    for l in range(depth):
        lam_init = 0.8 - 0.6 * math.exp(-0.3 * l)
        final = l == depth - 1
        wl = w_in[l]
        s0, s1 = 3 * cw, 3 * cw + dq
        wn = jnp.concatenate([wl[:, :s1], wl[:, s1 + dq:s1 + 2 * dq], wl[:, s1 + 2 * dq:s1 + 2 * dq + dfx]],
                             axis=1).astype(BF16)
        wt = jnp.concatenate([wl[:, s1:s1 + dq], wl[:, s1 + 2 * dq + dfx:s1 + 2 * dq + 3 * dfx]], axis=1).T.astype(BF16)
        wfl = _pad_rows(wl[:, s1 + 2 * dq + 3 * dfx:].T, 2 * SUBLANES).astype(BF16)
        bft = _pad_rows(b_f[l][:, None], SUBLANES)
        cwp = _pad_rows(conv_w[l], SUBLANES)
        fwp = _pad_rows(ffn_conv[l], SUBLANES)
        g1, g2, gf, gs = norm1[l][None], norm2[l][None], norm_f[None], subln[l][None]
        wo, wu, wd = w_out[l].astype(BF16), w_up[l].astype(BF16), w_down[l].astype(BF16)
        lam_p = diff_lambda[l]
        inproj = functools.partial(_inproj, da_scale=da_scale * LOG2E, fox_scale=fox_scale * LOG2E)

        (conv, cst, qz, v, vb, qfz, kt, ktb, kft, kftb, vft, vftb, lft, ct, ccol) = inproj(
            xp, g1, wn, wt, wfl, bft, cwp, zeros_c, tm=t, period=t, prompt=True)
        od = _diff_prompt(qz, ktb, vb, dtiles, lam_p, gs, t=t, lam_init=lam_init)
        of = _fox_prompt(qfz, kftb, vftb, ct, ccol, t=t)
        xp, fst = _outffn(xp, conv, od, of, wo, g2, wu, fwp, wd, gf, zeros_f,
                          tm=t, period=t, prompt=True, final=final, chunk_w=ffn_chunk)
        st_p.append((
            jnp.transpose(kt.reshape(B, n_da, 2, da_qk, S), (0, 4, 1, 2, 3)),
            v.reshape(B, S, n_da, da_v),
            jnp.transpose(kft.reshape(B, n_fox, fox_hd, S), (0, 3, 1, 2)),
            jnp.transpose(vft.reshape(B, n_fox, fox_hd, S), (0, 3, 1, 2)),
            jnp.swapaxes(lft[:, :n_fox, :], 1, 2),
            cst[:, SUBLANES - 2:, :], fst[:, SUBLANES - 2:, :]))

        pc = (jnp.repeat(state_conv[l][:, 0], T_NEW, axis=0)[None], jnp.repeat(state_conv[l][:, 1], T_NEW, axis=0)[None])
        pf = (jnp.repeat(state_ffn[l][:, 0], T_NEW, axis=0)[None], jnp.repeat(state_ffn[l][:, 1], T_NEW, axis=0)[None])
        (conv_s, gcu_s, qz_s, v_s, vb_s, qfz_s, kt_s, ktb_s, kft_s, kftb_s, vft_s, vftb_s, lft_s) = inproj(
            xs, g1, wn, wt, wfl, bft, cwp, pc, tm=rs, period=T_NEW, prompt=False)
        k_s = kt_s[0].T.reshape(DB, T_NEW, dq)
        kf_s = kft_s[0].T.reshape(DB, T_NEW, dfx)
        vf_s = vft_s[0].T.reshape(DB, T_NEW, dfx)
        lf_s = lft_s[0, :n_fox].T.reshape(DB, T_NEW, n_fox)
        pad_new = lambda a: jnp.pad(a, ((0, 0), (0, new_pad - T_NEW), (0, 0)))
        pad_lane = lambda a: jnp.pad(a, ((0, 0), (0, LANES - T_NEW), (0, 0)))
        q5 = jnp.transpose(qz_s.reshape(DB, T_NEW, n_da, 2, LANES), (0, 2, 3, 1, 4))
        qbd = (q5[:, :, :, :, None, :] * eye_da[None, :, None, None, :, None]).reshape(DB, n_da * 2 * T_NEW, dq)
        qf4 = jnp.transpose(qfz_s.reshape(DB, T_NEW, n_fox, LANES), (0, 2, 1, 3))
        qfbd = (qf4[:, :, :, None, :] * pair_of_head[None, :, None, :, None]).reshape(DB, n_fox * T_NEW, dfx)
        lnew = jnp.pad(jnp.broadcast_to(jnp.swapaxes(lf_s, 1, 2)[:, :, None, :], (DB, n_fox, T_NEW, T_NEW)),
                       ((0, 0), (0, 0), (0, 0), (0, LANES - T_NEW))).reshape(DB, n_fox * T_NEW, LANES)
        od_s, of_s = _decode(pt_flat, l, qbd, dkt, dv4, bias_past, pad_new(k_s).astype(BF16),
                             pad_new(vb_s.reshape(DB, T_NEW, dq)), bias_new, lam_p, gs,
                             qfbd, fkt, fvt, flt, pad_lane(kf_s).astype(BF16),
                             jnp.swapaxes(pad_lane(vf_s), 1, 2).astype(BF16), lnew,
                             pages=pages, n_pages=n_pages, lam_init=lam_init, dec_seq=T_NEW)
        od_s = jnp.transpose(od_s.reshape(DB, n_da, 2, T_NEW, da_v)[:, :, 0], (0, 2, 1, 3)).reshape(1, rs, dq)
        of5 = of_s.reshape(DB, n_fox, T_NEW, n_fox, fox_hd)
        of_s = jnp.transpose(jnp.einsum('bhthd->bhtd', of5), (0, 2, 1, 3)).reshape(1, rs, dfx)
        xs, up_s = _outffn(xs, conv_s, od_s.astype(BF16), of_s.astype(BF16), wo, g2, wu, fwp, wd, gf, pf,
                           tm=rs, period=T_NEW, prompt=False, final=final, chunk_w=ffn_chunk)
        st_s.append((
            k_s.reshape(DB, T_NEW, n_da, 2, da_qk), v_s.reshape(DB, T_NEW, n_da, da_v),
            kf_s.reshape(DB, T_NEW, n_fox, fox_hd), vf_s.reshape(DB, T_NEW, n_fox, fox_hd), lf_s,
            gcu_s.reshape(DB, T_NEW, cw)[:, T_NEW - 2:], up_s.reshape(DB, T_NEW, 2 * dff)[:, T_NEW - 2:]))

    outs = [xp, xs.reshape(DB, T_NEW, D)]
    for j in range(7):
        outs.append(jnp.stack([s[j] for s in st_p]))
        outs.append(jnp.stack([s[j] for s in st_s]))
    return tuple(outs)
```

```python
import functools
import math

import jax
import jax.numpy as jnp
from jax import lax
from jax.experimental import pallas as pl
from jax.experimental.pallas import tpu as pltpu

F32, BF16, I32 = jnp.float32, jnp.bfloat16, jnp.int32
RMS_EPS = 1e-6
NEG_INF = -1e30
NUM_BUCKETS = 32
MAX_DISTANCE = 128
LANES = 128
SUBLANES = 8
VMEM_LIMIT_BYTES = 56 * 2**20
NT_DIMS = (((1,), (1,)), ((), ()))
LOG2E = math.log2(math.e)
PROMPT_TILE = 512
DECODE_PAGES = 32
FFN_CHUNK = 256


def _cparams(n_axes):
    return pltpu.CompilerParams(dimension_semantics=("arbitrary",) * n_axes, vmem_limit_bytes=VMEM_LIMIT_BYTES)


def _dot(a, b):
    return jnp.dot(a, b, preferred_element_type=F32)


def _dot_nt(a, b):
    return lax.dot_general(a, b, NT_DIMS, preferred_element_type=F32)


def _rms(x, g):
    return x * lax.rsqrt(jnp.mean(x * x, axis=-1, keepdims=True) + RMS_EPS) * g


def _log_sigmoid(x):
    return jnp.minimum(x, 0.0) - jnp.log1p(jnp.exp(-jnp.abs(x)))


def _conv3(u, p0, p1, w_ref, cols, period):
    rows = u.shape[0]
    t = lax.broadcasted_iota(I32, (rows, 1), 0)
    if period < rows:
        t = jnp.bitwise_and(t, period - 1)
    sh1 = jnp.where(t == 0, p1, pltpu.roll(u, 1, 0))
    sh2 = jnp.where(t == 0, p0, jnp.where(t == 1, p1, pltpu.roll(u, 2, 0)))
    return sh2 * w_ref[0:1, cols] + sh1 * w_ref[1:2, cols] + u * w_ref[2:3, cols]


def _prefix_sum_lanes(y, lane):
    for s in (1, 2, 4, 8, 16, 32, 64):
        y = y + jnp.where(lane >= s, pltpu.roll(y, s, 1), 0.0)
    return y


def _put_leaf(ref, prev_ref, val):
    if len(ref.shape) == 4:
        n_prev = ref.shape[0] - 1
        if n_prev:
            ref[0:n_prev, 0] = prev_ref[:, 0]
        ref[n_prev, 0] = val
    else:
        ref[0] = val


def _inproj_kernel(*refs, tm, period, prompt, stacked, da_scale, fox_scale, n_da, n_pairs):
    pv_ref = pkt_ref = pkft_ref = pvft_ref = plft_ref = None
    if prompt:
        if stacked:
            pv_ref, pkt_ref, pkft_ref, pvft_ref, plft_ref = refs[8:13]
            refs = refs[:8] + refs[13:]
        (x_ref, g_ref, wn_ref, wt_ref, wfl_ref, bf_ref, cw_ref, pref_ref,
         conv_ref, st_ref, qz_ref, v_ref, vb_ref, qfz_ref, kt_ref, kft_ref, vft_ref, lft_ref,
         ktb_ref, kftb_ref, vftb_ref, ct_ref, ccol_ref, carry_ref, ccarry_ref) = refs
    else:
        (x_ref, g_ref, wn_ref, wt_ref, wfl_ref, bf_ref, cw_ref, p0_ref, p1_ref,
         conv_ref, st_ref, qz_ref, v_ref, vb_ref, qfz_ref, kt_ref, kft_ref, vft_ref, lft_ref) = refs
    i = pl.program_id(1)
    cw = cw_ref.shape[1]
    dq = n_da * LANES
    dfx = n_pairs * LANES
    h = _rms(x_ref[0], g_ref[...]).astype(BF16)

    ugc = _dot(h, wn_ref[:, 0:3 * cw])
    gcu = ugc[:, 2 * cw:3 * cw] * ugc[:, 0:cw]
    if prompt:
        @pl.when(i == 0)
        def _():
            carry_ref[...] = pref_ref[0]
            ccarry_ref[...] = jnp.zeros_like(ccarry_ref)
        p0, p1 = carry_ref[6:7, :], carry_ref[7:8, :]
    else:
        p0, p1 = p0_ref[0], p1_ref[0]
    cy = _conv3(gcu, p0, p1, cw_ref, slice(None), period)
    conv_ref[0] = (ugc[:, cw:2 * cw] * cy).astype(BF16)
    if prompt:
        carry_ref[...] = gcu[tm - SUBLANES:tm]
        st_ref[0] = gcu[tm - SUBLANES:tm]
    else:
        st_ref[0] = gcu

    lo = lax.broadcasted_iota(I32, (tm, LANES), 1) < (LANES // 2)
    c0 = 3 * cw
    q = _dot(h, wn_ref[:, c0:c0 + dq]) * da_scale
    for hh in range(n_da):
        qh = q[:, hh * LANES:(hh + 1) * LANES]
        qz_ref[0, :, 2 * hh * LANES:(2 * hh + 1) * LANES] = jnp.where(lo, qh, 0.0).astype(BF16)
        qz_ref[0, :, (2 * hh + 1) * LANES:(2 * hh + 2) * LANES] = jnp.where(lo, 0.0, qh).astype(BF16)
    c0 += dq
    v = _dot(h, wn_ref[:, c0:c0 + dq])
    _put_leaf(v_ref, pv_ref, v)
    vb_ref[0] = v.astype(BF16)
    c0 += dq
    qf = _dot(h, wn_ref[:, c0:c0 + dfx]) * fox_scale
    for p in range(n_pairs):
        qp = qf[:, p * LANES:(p + 1) * LANES]
        qfz_ref[0, :, 2 * p * LANES:(2 * p + 1) * LANES] = jnp.where(lo, qp, 0.0).astype(BF16)
        qfz_ref[0, :, (2 * p + 1) * LANES:(2 * p + 2) * LANES] = jnp.where(lo, 0.0, qp).astype(BF16)

    kt = _dot_nt(wt_ref[0:dq, :], h)
    _put_leaf(kt_ref, pkt_ref, kt)
    kft = _dot_nt(wt_ref[dq:dq + dfx, :], h)
    _put_leaf(kft_ref, pkft_ref, kft)
    vft = _dot_nt(wt_ref[dq + dfx:dq + 2 * dfx, :], h)
    _put_leaf(vft_ref, pvft_ref, vft)
    if prompt:
        ktb_ref[0, 0] = kt.astype(BF16)
        kftb_ref[0, 0] = kft.astype(BF16)
        vftb_ref[0, 0] = vft.astype(BF16)
    lf = _log_sigmoid(_dot_nt(wfl_ref[...], h)[0:SUBLANES] + bf_ref[...])
    _put_leaf(lft_ref, plft_ref, lf)

    if prompt:
        lane = lax.broadcasted_iota(I32, (SUBLANES, LANES), 1)
        carry = ccarry_ref[...]
        blocks = []
        for blk in range(tm // LANES):
            y = _prefix_sum_lanes(lf[:, blk * LANES:(blk + 1) * LANES], lane) + carry
            carry = jnp.broadcast_to(y[:, LANES - 1:LANES], (SUBLANES, LANES))
            blocks.append(y)
        ccarry_ref[...] = carry
        ct = jnp.concatenate(blocks, axis=1) * LOG2E
        ct_ref[0, 0] = ct
        ccol_ref[0] = jnp.concatenate([ct, jnp.zeros((LANES - SUBLANES, tm), F32)], axis=0).T


def _inproj(x, g1, wn, wt, wfl, bft, cwp, pre, *, tm, period, prompt, da_scale, fox_scale, prev=None):
    G, R, D = x.shape
    nt = R // tm
    cw = cwp.shape[1]
    dq = (wn.shape[1] - 3 * cw) * 2 // 5
    dfx = dq // 2
    n_da, n_pairs = dq // LANES, dfx // LANES
    row = lambda c: pl.BlockSpec((1, tm, c), lambda g, i: (g, i, 0))
    colT = lambda r: pl.BlockSpec((1, r, tm), lambda g, i: (g, 0, i))
    chunkT = lambda r: pl.BlockSpec((1, 1, r, tm), lambda g, i: (g, i, 0, 0))
    full = lambda a: pl.BlockSpec(a.shape, lambda g, i: (0,) * a.ndim)
    in_specs = [row(D), full(g1), full(wn), full(wt), full(wfl), full(bft), full(cwp)]
    if prompt:
        in_specs += [pl.BlockSpec((1, SUBLANES, cw), lambda g, i: (g, 0, 0))]
        ins = (x, g1, wn, wt, wfl, bft, cwp, pre)
        st_spec, st_shape = pl.BlockSpec((1, SUBLANES, cw), lambda g, i: (g, 0, 0)), (G, SUBLANES, cw)
        n_prev = prev[0].shape[0] if prev else 0
        rows_l = lambda n, c: pl.BlockSpec((n, 1, tm, c), lambda g, i: (0, g, i, 0))
        cols_l = lambda n, r: pl.BlockSpec((n, 1, r, tm), lambda g, i: (0, g, 0, i))
        if prev:
            in_specs += [rows_l(n_prev, dq)] + [cols_l(n_prev, r) for r in (dq, dfx, dfx, SUBLANES)]
            ins += tuple(prev)
        L = n_prev + 1
        leaves = [(rows_l(L, dq), jax.ShapeDtypeStruct((L, G, R, dq), F32))]
        leaves += [(cols_l(L, r), jax.ShapeDtypeStruct((L, G, r, R), F32)) for r in (dq, dfx, dfx, SUBLANES)]
    else:
        in_specs += [row(cw), row(cw)]
        ins = (x, g1, wn, wt, wfl, bft, cwp, pre[0], pre[1])
        st_spec, st_shape = row(cw), (G, R, cw)
        leaves = [(row(dq), jax.ShapeDtypeStruct((G, R, dq), F32))]
        leaves += [(colT(r), jax.ShapeDtypeStruct((G, r, R), F32)) for r in (dq, dfx, dfx, SUBLANES)]
    outs = [
        (row(cw), jax.ShapeDtypeStruct((G, R, cw), BF16)),
        (st_spec, jax.ShapeDtypeStruct(st_shape, F32)),
        (row(2 * dq), jax.ShapeDtypeStruct((G, R, 2 * dq), BF16)),
        leaves[0],
        (row(dq), jax.ShapeDtypeStruct((G, R, dq), BF16)),
        (row(2 * dfx), jax.ShapeDtypeStruct((G, R, 2 * dfx), BF16)),
    ] + leaves[1:]
    scratch = []
    if prompt:
        outs += [(chunkT(r), jax.ShapeDtypeStruct((G, nt, r, tm), BF16)) for r in (dq, dfx, dfx)]
        outs += [
            (chunkT(SUBLANES), jax.ShapeDtypeStruct((G, nt, SUBLANES, tm), F32)),
            (row(LANES), jax.ShapeDtypeStruct((G, R, LANES), F32)),
        ]
        scratch = [pltpu.VMEM((SUBLANES, cw), F32), pltpu.VMEM((SUBLANES, LANES), F32)]
    kern = functools.partial(_inproj_kernel, tm=tm, period=period, prompt=prompt, stacked=bool(prev),
                             da_scale=da_scale, fox_scale=fox_scale, n_da=n_da, n_pairs=n_pairs)
    return pl.pallas_call(
        kern, grid=(G, nt), in_specs=in_specs, out_specs=[o[0] for o in outs],
        out_shape=[o[1] for o in outs], scratch_shapes=scratch, compiler_params=_cparams(2),
        name="inproj_prompt" if prompt else "inproj_sample")(*ins)


def _bias_of_distance(n, tab_ref, h, n_heads):
    max_exact = NUM_BUCKETS // 2
    nf = jnp.maximum(n, 1).astype(F32)
    large = max_exact + (jnp.log(nf / max_exact) / math.log(MAX_DISTANCE / max_exact)
                         * (NUM_BUCKETS - max_exact)).astype(I32)
    bucket = jnp.where(n < max_exact, n, jnp.minimum(large, NUM_BUCKETS - 1))
    far = tab_ref[(NUM_BUCKETS - 1) * n_heads + h]
    out = jnp.zeros(n.shape, F32)
    for b in range(NUM_BUCKETS - 1):
        out = jnp.where(bucket == b, (tab_ref[b * n_heads + h] - far) * LOG2E, out)
    return out


def _bias_kernel(tab_ref, d_ref, bp_ref, bn_ref, *, t, past, n_heads, new_pad, dec_seq):
    r = lax.broadcasted_iota(I32, (t, t), 0)
    c = lax.broadcasted_iota(I32, (t, t), 1)
    rp = lax.broadcasted_iota(I32, (SUBLANES, past), 0)
    kp = lax.broadcasted_iota(I32, (SUBLANES, past), 1)
    rn = lax.broadcasted_iota(I32, (SUBLANES, new_pad), 0)
    kn = lax.broadcasted_iota(I32, (SUBLANES, new_pad), 1)
    tp = jnp.bitwise_and(rp, dec_seq - 1)
    tn = jnp.bitwise_and(rn, dec_seq - 1)
    for h in range(n_heads):
        d_ref[h, 0] = jnp.where(r >= c, _bias_of_distance(jnp.maximum(r - c, 0), tab_ref, h, n_heads), NEG_INF)
        d_ref[h, 1] = _bias_of_distance(t + r - c, tab_ref, h, n_heads)
        bp_ref[h * SUBLANES:(h + 1) * SUBLANES, :] = _bias_of_distance(past + tp - kp, tab_ref, h, n_heads)
        bn_ref[h * SUBLANES:(h + 1) * SUBLANES, :] = jnp.where(
            kn <= tn, _bias_of_distance(jnp.maximum(tn - kn, 0), tab_ref, h, n_heads), NEG_INF)


def _bias_tiles(rel_bias, *, t, past, new_pad, dec_seq):
    n_heads = rel_bias.shape[1]
    kern = functools.partial(_bias_kernel, t=t, past=past, n_heads=n_heads, new_pad=new_pad, dec_seq=dec_seq)
    return pl.pallas_call(
        kern,
        in_specs=[pl.BlockSpec(memory_space=pltpu.SMEM)],
        out_specs=[pl.BlockSpec(memory_space=pltpu.VMEM)] * 3,
        out_shape=[jax.ShapeDtypeStruct((n_heads, 2, t, t), F32),
                   jax.ShapeDtypeStruct((n_heads * SUBLANES, past), F32),
                   jax.ShapeDtypeStruct((n_heads * SUBLANES, new_pad), F32)],
        compiler_params=pltpu.CompilerParams(vmem_limit_bytes=VMEM_LIMIT_BYTES),
        name="rel_bias_tiles")(rel_bias.reshape(-1))


def _lambda_of(lp, lam_init):
    a = jnp.sum(lp[0:1] * lp[1:2], axis=-1, keepdims=True)
    b = jnp.sum(lp[2:3] * lp[3:4], axis=-1, keepdims=True)
    return jnp.exp(a) - jnp.exp(b) + lam_init


def _flash_update(s, m_ref, acc_ref, pv, row_off=None):
    blocks = [s[:, c * LANES:(c + 1) * LANES] for c in range(s.shape[1] // LANES)]
    rmax = jnp.max(functools.reduce(jnp.maximum, blocks), axis=-1, keepdims=True)
    m_old = m_ref[...]
    m_new = jnp.maximum(m_old, rmax if row_off is None else rmax + row_off)
    alpha = jnp.exp2(m_old - m_new)
    shift = m_new if row_off is None else m_new - row_off
    p = jnp.concatenate([jnp.exp2(b - shift).astype(BF16) for b in blocks], axis=1)
    acc_ref[...] = jnp.concatenate([alpha, alpha], axis=1) * acc_ref[...] + pv(p)
    m_ref[...] = m_new


CHUNKS_PER_BLOCK = 8


def _run_chunks(n, chunk, last):
    def group(jg, carry):
        for u in range(CHUNKS_PER_BLOCK):
            chunk(CHUNKS_PER_BLOCK * jg + u)
        return carry

    n_groups = n // CHUNKS_PER_BLOCK
    lax.fori_loop(0, n_groups, group, 0)
    base = n_groups * CHUNKS_PER_BLOCK
    for rem in range(CHUNKS_PER_BLOCK):
        @pl.when(n - base == rem)
        def _():
            for u in range(rem):
                chunk(base + u)
            last()


def _diff_prompt_kernel(q_ref, k_ref, v_ref, d_ref, lam_ref, g_ref, o_ref, m1, a1, m2, a2, *, t, lam_init):
    i = pl.program_id(2)
    for m, a in ((m1, a1), (m2, a2)):
        m[...] = jnp.full_like(m, NEG_INF)
        a[...] = jnp.zeros_like(a)
    q1, q2 = q_ref[0, :, 0:LANES], q_ref[0, :, LANES:2 * LANES]

    corner = d_ref[0, 1, 0:LANES, t - LANES:t]

    def chunk(j, diagonal=False):
        kt = k_ref[0, j]
        v = v_ref[0, pl.ds(pl.multiple_of(j * t, t), t), :]
        v1 = jnp.concatenate([v, jnp.ones_like(v)], axis=1)
        for q, m, a in ((q1, m1, a1), (q2, m2, a2)):
            s = _dot(q, kt)
            if diagonal:
                s = s + d_ref[0, 0]
            else:
                near = s[0:LANES, t - LANES:] + jnp.where(j == i - 1, corner, 0.0)
                if t > LANES:
                    top = jnp.concatenate([s[0:LANES, :t - LANES], near], axis=1)
                    s = jnp.concatenate([top, s[LANES:]], axis=0)
                else:
                    s = near
            _flash_update(s, m, a, lambda p: _dot(p, v1))

    _run_chunks(i, chunk, lambda: chunk(i, diagonal=True))

    lam = _lambda_of(lam_ref[...], lam_init)
    o = a1[:, 0:LANES] / a1[:, LANES:] - lam * (a2[:, 0:LANES] / a2[:, LANES:])
    o_ref[0] = (_rms(o, g_ref[...]) * (1.0 - lam_init)).astype(BF16)


def _diff_prompt(qz, ktb, vb, dtiles, lam_p, subln, *, t, lam_init):
    B, S, _ = qz.shape
    H = dtiles.shape[0]
    nq = S // t
    kern = functools.partial(_diff_prompt_kernel, t=t, lam_init=lam_init)
    return pl.pallas_call(
        kern, grid=(B, H, nq),
        in_specs=[pl.BlockSpec((1, t, 2 * LANES), lambda b, h, i: (b, i, h)),
                  pl.BlockSpec((1, nq, LANES, t), lambda b, h, i: (b, 0, h, 0)),
                  pl.BlockSpec((1, S, LANES), lambda b, h, i: (b, 0, h)),
                  pl.BlockSpec((1, 2, t, t), lambda b, h, i: (h, 0, 0, 0)),
                  pl.BlockSpec(lam_p.shape, lambda b, h, i: (0, 0)),
                  pl.BlockSpec(subln.shape, lambda b, h, i: (0, 0))],
        out_specs=pl.BlockSpec((1, t, LANES), lambda b, h, i: (b, i, h)),
        out_shape=jax.ShapeDtypeStruct((B, S, H * LANES), BF16),
        scratch_shapes=[pltpu.VMEM((t, LANES), F32), pltpu.VMEM((t, 2 * LANES), F32)] * 2,
        compiler_params=_cparams(3), name="diff_attn_prompt")(qz, ktb, vb, dtiles, lam_p, subln)


def _fox_prompt_kernel(q_ref, k_ref, v_ref, c_ref, cc_ref, o_ref, m0, a0, m1, a1, *, t):
    pr = pl.program_id(1)
    i = pl.program_id(2)
    state = ((m0, a0), (m1, a1))
    for m, a in state:
        m[...] = jnp.full_like(m, NEG_INF)
        a[...] = jnp.zeros_like(a)
    lane = lax.broadcasted_iota(I32, (t, LANES), 1)
    ccol = cc_ref[0]
    heads = []
    for e in range(2):
        hsel = lane == (2 * pr + e)
        cq = jnp.sum(jnp.where(hsel, ccol, 0.0), axis=-1, keepdims=True)
        cbase = cq[0:1, :]
        heads.append((q_ref[0, :, e * LANES:(e + 1) * LANES], jnp.broadcast_to(cq - cbase, (t, LANES)), cbase, state[e]))

    def chunk(j, diag):
        kt = k_ref[0, j]
        vt = v_ref[0, j]
        vt1 = jnp.concatenate([vt, jnp.ones_like(vt)], axis=0)
        crow = c_ref[0, j]
        crow_idx = lax.broadcasted_iota(I32, crow.shape, 0)
        for e, (q, cqr, cbase, (m, a)) in enumerate(heads):
            ck = jnp.sum(jnp.where(crow_idx == 2 * pr + e, crow, 0.0), axis=0, keepdims=True)
            z = _dot(q, kt) - (ck - cbase)
            if diag:
                r = lax.broadcasted_iota(I32, (t, t), 0)
                c = lax.broadcasted_iota(I32, (t, t), 1)
                z = jnp.where(r >= c, z, NEG_INF)
            _flash_update(z, m, a, lambda p: _dot_nt(p, vt1), row_off=cqr)

    _run_chunks(i, lambda j: chunk(j, False), lambda: chunk(i, True))
    o_ref[0] = jnp.where(lane < LANES // 2, a0[:, 0:LANES] / a0[:, LANES:], a1[:, 0:LANES] / a1[:, LANES:]).astype(BF16)


def _fox_prompt(qfz, kftb, vftb, ct, ccol, *, t):
    B, S, _ = qfz.shape
    nq = S // t
    n_pairs = kftb.shape[2] // LANES
    kern = functools.partial(_fox_prompt_kernel, t=t)
    return pl.pallas_call(
        kern, grid=(B, n_pairs, nq),
        in_specs=[pl.BlockSpec((1, t, 2 * LANES), lambda b, p, i: (b, i, p)),
                  pl.BlockSpec((1, nq, LANES, t), lambda b, p, i: (b, 0, p, 0)),
                  pl.BlockSpec((1, nq, LANES, t), lambda b, p, i: (b, 0, p, 0)),
                  pl.BlockSpec((1, nq, SUBLANES, t), lambda b, p, i: (b, 0, 0, 0)),
                  pl.BlockSpec((1, t, LANES), lambda b, p, i: (b, i, 0))],
        out_specs=pl.BlockSpec((1, t, LANES), lambda b, p, i: (b, i, p)),
        out_shape=jax.ShapeDtypeStruct((B, S, n_pairs * LANES), BF16),
        scratch_shapes=[pltpu.VMEM((t, LANES), F32), pltpu.VMEM((t, 2 * LANES), F32)] * 2,
        compiler_params=_cparams(3), name="fox_attn_prompt")(qfz, kftb, vftb, ct, ccol)


def _outffn_kernel(*refs, tm, period, prompt, final, chunk_w):
    if prompt:
        (x_ref, cv_ref, od_ref, of_ref, wo_ref, g2_ref, wu_ref, fw_ref, wd_ref, gf_ref, pref_ref,
         xo_ref, st_ref, act_ref, carry_ref) = refs
    else:
        (x_ref, cv_ref, od_ref, of_ref, wo_ref, g2_ref, wu_ref, fw_ref, wd_ref, gf_ref, p0_ref, p1_ref,
         xo_ref, st_ref, act_ref) = refs
    i = pl.program_id(1)
    cw, dw = cv_ref.shape[2], od_ref.shape[2]
    dff = wd_ref.shape[0]
    x1 = (x_ref[0] + _dot(cv_ref[0], wo_ref[0:cw, :]) + _dot(od_ref[0], wo_ref[cw:cw + dw, :])
          + _dot(of_ref[0], wo_ref[cw + dw:, :]))
    xn = _rms(x1, g2_ref[...]).astype(BF16)
    if prompt:
        @pl.when(i == 0)
        def _():
            carry_ref[...] = pref_ref[0]
    for c in range(dff // chunk_w):
        halves = []
        for base in (c * chunk_w, dff + c * chunk_w):
            cols = slice(base, base + chunk_w)
            up = _dot(xn, wu_ref[:, cols])
            if prompt:
                p0, p1 = carry_ref[6:7, cols], carry_ref[7:8, cols]
            else:
                p0, p1 = p0_ref[0, :, cols], p1_ref[0, :, cols]
            halves.append(_conv3(up, p0, p1, fw_ref, cols, period))
            if prompt:
                carry_ref[:, cols] = up[tm - SUBLANES:tm]
                st_ref[0, :, cols] = up[tm - SUBLANES:tm]
            else:
                st_ref[0, :, cols] = up
        val, gate = halves
        act = gate * (1.0 / (1.0 + jnp.exp(-gate))) * val
        act_ref[:, c * chunk_w:(c + 1) * chunk_w] = act.astype(BF16)
    x2 = x1 + _dot(act_ref[...], wd_ref[...])
    xo_ref[0] = _rms(x2, gf_ref[...]) if final else x2


def _outffn(x, conv, od, of, wo, g2, wu, fw, wd, gf, pre, *, tm, period, prompt, final, chunk_w):
    G, R, D = x.shape
    nt = R // tm
    dff2 = wu.shape[1]
    row = lambda c: pl.BlockSpec((1, tm, c), lambda g, i: (g, i, 0))
    once = lambda a: pl.BlockSpec(a.shape, lambda g, i: (0,) * a.ndim, pipeline_mode=pl.Buffered(1))
    in_specs = [row(D), row(conv.shape[2]), row(od.shape[2]), row(of.shape[2]),
                once(wo), once(g2), once(wu), once(fw), once(wd), once(gf)]
    if prompt:
        in_specs += [pl.BlockSpec((1, SUBLANES, dff2), lambda g, i: (g, 0, 0))]
        ins = (x, conv, od, of, wo, g2, wu, fw, wd, gf, pre)
        st_spec, st_shape = pl.BlockSpec((1, SUBLANES, dff2), lambda g, i: (g, 0, 0)), (G, SUBLANES, dff2)
        scratch = [pltpu.VMEM((tm, dff2 // 2), BF16), pltpu.VMEM((SUBLANES, dff2), F32)]
    else:
        in_specs += [row(dff2), row(dff2)]
        ins = (x, conv, od, of, wo, g2, wu, fw, wd, gf, pre[0], pre[1])
        st_spec, st_shape = row(dff2), (G, R, dff2)
        scratch = [pltpu.VMEM((tm, dff2 // 2), BF16)]
    kern = functools.partial(_outffn_kernel, tm=tm, period=period, prompt=prompt, final=final, chunk_w=chunk_w)
    return pl.pallas_call(
        kern, grid=(G, nt), in_specs=in_specs, out_specs=[row(D), st_spec],
        out_shape=[jax.ShapeDtypeStruct((G, R, D), F32), jax.ShapeDtypeStruct(st_shape, F32)],
        scratch_shapes=scratch, compiler_params=_cparams(2),
        name="outffn_prompt" if prompt else "outffn_sample")(*ins)


def _diff_decode_kernel(pt_ref, q_ref, *refs, pages, n_heads, lam_init, dec_seq):
    kp = refs[0:pages]
    vp = refs[pages:2 * pages]
    bp_ref, kn_ref, vn_ref, bn_ref, lam_ref, g_ref, o_ref, m_ref, l_ref, a_ref = refs[2 * pages:]
    c = pl.program_id(1)
    q = q_ref[0]
    rows = q.shape[0]

    def update(s, pv):
        m_old = m_ref[...]
        m_new = jnp.maximum(m_old, jnp.max(s, axis=-1, keepdims=True))
        alpha = jnp.exp2(m_old - m_new)
        p = jnp.exp2(s - m_new)
        l_ref[...] = alpha * l_ref[...] + jnp.sum(p, axis=-1, keepdims=True)
        a_ref[...] = alpha * a_ref[...] + pv(p.astype(BF16))
        m_ref[...] = m_new

    def per_head_pv(p, v_of_head):
        outs = []
        for h in range(n_heads):
            g = (h // 2) * 2 * SUBLANES
            res = _dot(p[g:g + 2 * SUBLANES, :], v_of_head(h))
            outs.append(res[(h % 2) * SUBLANES:(h % 2 + 1) * SUBLANES, :])
        return jnp.concatenate(outs, axis=0)

    @pl.when(c == 0)
    def _():
        m_ref[...] = jnp.full_like(m_ref, NEG_INF)
        l_ref[...] = jnp.zeros_like(l_ref)
        a_ref[...] = jnp.zeros_like(a_ref)
        vn = vn_ref[0]
        update(_dot_nt(q, kn_ref[0]) + bn_ref[...],
               lambda p: per_head_pv(p, lambda h: vn[:, h * LANES:(h + 1) * LANES]))

    kt = jnp.concatenate([r[0, 0] for r in kp], axis=1).astype(BF16)
    s = _dot(q, kt) + bp_ref[...]

    def v_of_head(h):
        return jnp.concatenate([r[0, 0, pl.ds(h, LANES, stride=n_heads), :] for r in vp], axis=0).astype(BF16)

    update(s, lambda p: per_head_pv(p, v_of_head))

    @pl.when(c == pl.num_programs(1) - 1)
    def _():
        lam = _lambda_of(lam_ref[...], lam_init)
        on = a_ref[...] / l_ref[...]
        o = on - lam * pltpu.roll(on, rows - dec_seq, 0)
        o_ref[0] = _rms(o, g_ref[...]) * (1.0 - lam_init)


def _fox_decode_kernel(pt_ref, q_ref, *refs, pages, n_heads, dec_seq):
    kp = refs[0:pages]
    vp = refs[pages:2 * pages]
    fp = refs[2 * pages:3 * pages]
    kn_ref, vn_ref, ln_ref, o_ref, m_ref, l_ref, a_ref, cq_ref, run_ref = refs[3 * pages:]
    c = pl.program_id(1)
    q = q_ref[0]
    rows = q.shape[0]

    def update(z, vt):
        m_old = m_ref[...]
        m_new = jnp.maximum(m_old, jnp.max(z, axis=-1, keepdims=True))
        alpha = jnp.exp2(m_old - m_new)
        p = jnp.exp2(z - m_new)
        l_ref[...] = alpha * l_ref[...] + jnp.sum(p, axis=-1, keepdims=True)
        a_ref[...] = alpha * a_ref[...] + _dot_nt(p.astype(BF16), vt)
        m_ref[...] = m_new

    @pl.when(c == 0)
    def _():
        m_ref[...] = jnp.full_like(m_ref, NEG_INF)
        l_ref[...] = jnp.zeros_like(l_ref)
        a_ref[...] = jnp.zeros_like(a_ref)
        run_ref[...] = jnp.zeros_like(run_ref)
        lane = lax.broadcasted_iota(I32, (rows, LANES), 1)
        trow = jnp.bitwise_and(lax.broadcasted_iota(I32, (rows, LANES), 0), dec_seq - 1)
        cnew = ln_ref[0] * LOG2E
        for s_ in (1, 2):
            cnew = cnew + jnp.where(lane >= s_, pltpu.roll(cnew, s_, 1), 0.0)
        cq = jnp.sum(jnp.where(lane == trow, cnew, 0.0), axis=-1, keepdims=True)
        cq_ref[...] = cq
        z = jnp.where(lane <= trow, _dot_nt(q, kn_ref[0]) + cq - cnew, NEG_INF)
        update(z, vn_ref[0])

    lane8 = lax.broadcasted_iota(I32, (SUBLANES, LANES), 1)
    later = run_ref[...]
    r_pages = [None] * pages
    for jj in reversed(range(pages)):
        y = _prefix_sum_lanes(fp[jj][0, 0] * LOG2E, lane8)
        tot = jnp.broadcast_to(y[:, LANES - 1:LANES], (SUBLANES, LANES))
        r_pages[jj] = tot - y + later
        later = later + tot
    run_ref[...] = later
    r8 = jnp.concatenate(r_pages, axis=1)
    row_head = lax.shift_right_logical(lax.broadcasted_iota(I32, (rows, 1), 0), dec_seq.bit_length() - 1)
    r16 = jnp.zeros((rows, r8.shape[1]), F32)
    for h in range(n_heads):
        r16 = jnp.where(row_head == h, r8[h:h + 1, :], r16)
    kt = jnp.concatenate([r[0, 0] for r in kp], axis=1).astype(BF16)
    vt = jnp.concatenate([r[0, 0] for r in vp], axis=1).astype(BF16)
    update(_dot(q, kt) + r16 + cq_ref[...], vt)

    @pl.when(c == pl.num_programs(1) - 1)
    def _():
        o_ref[0] = a_ref[...] / l_ref[...]


def _decode_kernel(pt_ref, *refs, pages, n_da, n_fox, lam_init, dec_seq):
    n_d, n_f = 1 + 2 * pages + 6, 1 + 3 * pages + 3
    d_in, f_in = refs[:n_d], refs[n_d:n_d + n_f]
    od_ref, of_ref = refs[n_d + n_f:n_d + n_f + 2]
    sc = refs[n_d + n_f + 2:]
    _diff_decode_kernel(pt_ref, *d_in, od_ref, *sc[0:3], pages=pages, n_heads=n_da, lam_init=lam_init, dec_seq=dec_seq)
    _fox_decode_kernel(pt_ref, *f_in, of_ref, *sc[3:8], pages=pages, n_heads=n_fox, dec_seq=dec_seq)


def _decode(pt_flat, layer, qbd, dkt, dv4, bias_past, knew, vnew, bias_new, lam_p, subln,
            qfbd, fkt, fvt, flt, kfnew, vfnew_t, lnew, *, pages, n_pages, lam_init, dec_seq):
    DB, drows, dq = qbd.shape
    _, frows, dfx = qfbd.shape
    nc = n_pages // pages
    per_seq = lambda a: pl.BlockSpec((1,) + a.shape[1:], lambda b, c, pt: (b, 0, 0))
    shared = lambda a: pl.BlockSpec(a.shape, lambda b, c, pt: (0, 0))

    def page_spec(jj, r, newest_first):
        chunk = (lambda c: nc - 1 - c) if newest_first else (lambda c: c)
        return pl.BlockSpec((1, 1, r, LANES),
                            lambda b, c, pt: (layer, pt[b * n_pages + chunk(c) * pages + jj], 0, 0))

    in_specs = ([per_seq(qbd)]
                + [page_spec(jj, dq, False) for jj in range(pages)] * 2
                + [pl.BlockSpec((drows, pages * LANES), lambda b, c, pt: (0, c)),
                   per_seq(knew), per_seq(vnew), shared(bias_new), shared(lam_p), shared(subln)]
                + [per_seq(qfbd)]
                + [page_spec(jj, dfx, True) for jj in range(pages)] * 2
                + [page_spec(jj, SUBLANES, True) for jj in range(pages)]
                + [per_seq(kfnew), per_seq(vfnew_t), per_seq(lnew)])
    kern = functools.partial(_decode_kernel, pages=pages, n_da=dq // LANES, n_fox=frows // dec_seq,
                             lam_init=lam_init, dec_seq=dec_seq)
    return pl.pallas_call(
        kern,
        grid_spec=pltpu.PrefetchScalarGridSpec(
            num_scalar_prefetch=1, grid=(DB, nc), in_specs=in_specs,
            out_specs=[pl.BlockSpec((1, drows, LANES), lambda b, c, pt: (b, 0, 0)),
                       pl.BlockSpec((1, frows, dfx), lambda b, c, pt: (b, 0, 0))],
            scratch_shapes=[pltpu.VMEM((drows, 1), F32), pltpu.VMEM((drows, 1), F32), pltpu.VMEM((drows, LANES), F32),
                            pltpu.VMEM((frows, 1), F32), pltpu.VMEM((frows, 1), F32), pltpu.VMEM((frows, dfx), F32),
                            pltpu.VMEM((frows, 1), F32), pltpu.VMEM((SUBLANES, LANES), F32)]),
        out_shape=[jax.ShapeDtypeStruct((DB, drows, LANES), F32), jax.ShapeDtypeStruct((DB, frows, dfx), F32)],
        compiler_params=_cparams(2), name="attn_decode",
    )(pt_flat, qbd, *([dkt] * pages), *([dv4] * pages), bias_past, knew, vnew, bias_new, lam_p, subln,
      qfbd, *([fkt] * pages), *([fvt] * pages), *([flt] * pages), kfnew, vfnew_t, lnew)


def _pad_rows(a, rows):
    return jnp.pad(a, ((0, rows - a.shape[0]),) + ((0, 0),) * (a.ndim - 1))


def _tile(n, pref):
    return pref if n % pref == 0 else n


def kernel(x_prompt, x_sample, cache_dk, cache_dv, cache_fk, cache_fv, cache_flogf, state_conv, state_ffn, page_table, rel_bias, norm1, w_in, b_f, conv_w, diff_lambda, subln, w_out, norm2, w_up, ffn_conv, w_down, norm_f):
    B, S, D = x_prompt.shape
    DB, T_NEW, _ = x_sample.shape
    depth, n_phys, page, n_da, _, da_qk = cache_dk.shape
    da_v = cache_dv.shape[-1]
    n_fox, fox_hd = cache_fk.shape[3], cache_fk.shape[4]
    cw = conv_w.shape[-1]
    dff = w_down.shape[1]
    n_pages = page_table.shape[1]
    past = n_pages * page
    dq, dfx = n_da * 2 * da_qk, n_fox * fox_hd
    assert page == LANES and 2 * da_qk == LANES and da_v == LANES and 2 * fox_hd == LANES
    assert n_fox % 2 == 0 and T_NEW & (T_NEW - 1) == 0 and T_NEW <= SUBLANES // 2
    assert (DB * T_NEW) % LANES == 0 and S % LANES == 0
    da_scale, fox_scale = da_qk ** -0.5, fox_hd ** -0.5
    t = _tile(S, PROMPT_TILE)
    rs = DB * T_NEW
    pages = _tile(n_pages, DECODE_PAGES)
    new_pad = 2 * SUBLANES
    ffn_chunk = _tile(dff, FFN_CHUNK)

    dkt = jnp.transpose(cache_dk, (0, 1, 3, 4, 5, 2)).reshape(depth, n_phys, dq, page)
    dv4 = cache_dv.reshape(depth, n_phys, page * n_da, da_v)
    fkt = jnp.transpose(cache_fk, (0, 1, 3, 4, 2)).reshape(depth, n_phys, dfx, page)
    fvt = jnp.transpose(cache_fv, (0, 1, 3, 4, 2)).reshape(depth, n_phys, dfx, page)
    flt = jnp.pad(jnp.swapaxes(cache_flogf, 2, 3), ((0, 0), (0, 0), (0, SUBLANES - n_fox), (0, 0)))
    pt_flat = page_table.reshape(-1)

    dtiles, bias_past, bias_new = _bias_tiles(rel_bias, t=t, past=past, new_pad=new_pad, dec_seq=T_NEW)

    xp = x_prompt
    xs = x_sample.reshape(1, rs, D)
    zeros_c = jnp.zeros((B, SUBLANES, cw), F32)
    zeros_f = jnp.zeros((B, SUBLANES, 2 * dff), F32)
    eye_da = jnp.eye(n_da, dtype=BF16)
    pair_of_head = (jnp.arange(n_fox)[:, None] // 2 == jnp.arange(n_fox // 2)[None, :]).astype(BF16)
    st_p, st_s, leaves_p = [], [], None
    for l in range(depth):
        lam_init = 0.8 - 0.6 * math.exp(-0.3 * l)
        final = l == depth - 1
        wl = w_in[l]
        s0, s1 = 3 * cw, 3 * cw + dq
        wn = jnp.concatenate([wl[:, :s1], wl[:, s1 + dq:s1 + 2 * dq], wl[:, s1 + 2 * dq:s1 + 2 * dq + dfx]],
                             axis=1).astype(BF16)
        wt = jnp.concatenate([wl[:, s1:s1 + dq], wl[:, s1 + 2 * dq + dfx:s1 + 2 * dq + 3 * dfx]], axis=1).T.astype(BF16)
        wfl = _pad_rows(wl[:, s1 + 2 * dq + 3 * dfx:].T, 2 * SUBLANES).astype(BF16)
        bft = _pad_rows(b_f[l][:, None], SUBLANES)
        cwp = _pad_rows(conv_w[l], SUBLANES)
        fwp = _pad_rows(ffn_conv[l], SUBLANES)
        g1, g2, gf, gs = norm1[l][None], norm2[l][None], norm_f[None], subln[l][None]
        wo, wu, wd = w_out[l].astype(BF16), w_up[l].astype(BF16), w_down[l].astype(BF16)
        lam_p = diff_lambda[l]
        inproj = functools.partial(_inproj, da_scale=da_scale * LOG2E, fox_scale=fox_scale * LOG2E)

        (conv, cst, qz, v, vb, qfz, kt, kft, vft, lft, ktb, kftb, vftb, ct, ccol) = inproj(
            xp, g1, wn, wt, wfl, bft, cwp, zeros_c, tm=t, period=t, prompt=True, prev=leaves_p)
        leaves_p = (v, kt, kft, vft, lft)
        od = _diff_prompt(qz, ktb, vb, dtiles, lam_p, gs, t=t, lam_init=lam_init)
        of = _fox_prompt(qfz, kftb, vftb, ct, ccol, t=t)
        xp, fst = _outffn(xp, conv, od, of, wo, g2, wu, fwp, wd, gf, zeros_f,
                          tm=t, period=t, prompt=True, final=final, chunk_w=ffn_chunk)
        st_p.append((cst[:, SUBLANES - 2:, :], fst[:, SUBLANES - 2:, :]))

        pc = (jnp.repeat(state_conv[l][:, 0], T_NEW, axis=0)[None], jnp.repeat(state_conv[l][:, 1], T_NEW, axis=0)[None])
        pf = (jnp.repeat(state_ffn[l][:, 0], T_NEW, axis=0)[None], jnp.repeat(state_ffn[l][:, 1], T_NEW, axis=0)[None])
        (conv_s, gcu_s, qz_s, v_s, vb_s, qfz_s, kt_s, kft_s, vft_s, lft_s) = inproj(
            xs, g1, wn, wt, wfl, bft, cwp, pc, tm=rs, period=T_NEW, prompt=False)
        k_s = kt_s[0].T.reshape(DB, T_NEW, dq)
        kf_s = kft_s[0].T.reshape(DB, T_NEW, dfx)
        vf_s = vft_s[0].T.reshape(DB, T_NEW, dfx)
        lf_s = lft_s[0, :n_fox].T.reshape(DB, T_NEW, n_fox)
        pad_new = lambda a: jnp.pad(a, ((0, 0), (0, new_pad - T_NEW), (0, 0)))
        pad_lane = lambda a: jnp.pad(a, ((0, 0), (0, LANES - T_NEW), (0, 0)))
        q5 = jnp.transpose(qz_s.reshape(DB, T_NEW, n_da, 2, LANES), (0, 2, 3, 1, 4))
        qbd = (q5[:, :, :, :, None, :] * eye_da[None, :, None, None, :, None]).reshape(DB, n_da * 2 * T_NEW, dq)
        qf4 = jnp.transpose(qfz_s.reshape(DB, T_NEW, n_fox, LANES), (0, 2, 1, 3))
        qfbd = (qf4[:, :, :, None, :] * pair_of_head[None, :, None, :, None]).reshape(DB, n_fox * T_NEW, dfx)
        lnew = jnp.pad(jnp.broadcast_to(jnp.swapaxes(lf_s, 1, 2)[:, :, None, :], (DB, n_fox, T_NEW, T_NEW)),
                       ((0, 0), (0, 0), (0, 0), (0, LANES - T_NEW))).reshape(DB, n_fox * T_NEW, LANES)
        od_s, of_s = _decode(pt_flat, l, qbd, dkt, dv4, bias_past, pad_new(k_s).astype(BF16),
                             pad_new(vb_s.reshape(DB, T_NEW, dq)), bias_new, lam_p, gs,
                             qfbd, fkt, fvt, flt, pad_lane(kf_s).astype(BF16),
                             jnp.swapaxes(pad_lane(vf_s), 1, 2).astype(BF16), lnew,
                             pages=pages, n_pages=n_pages, lam_init=lam_init, dec_seq=T_NEW)
        od_s = jnp.transpose(od_s.reshape(DB, n_da, 2, T_NEW, da_v)[:, :, 0], (0, 2, 1, 3)).reshape(1, rs, dq)
        of5 = of_s.reshape(DB, n_fox, T_NEW, n_fox, fox_hd)
        of_s = jnp.transpose(jnp.einsum('bhthd->bhtd', of5), (0, 2, 1, 3)).reshape(1, rs, dfx)
        xs, up_s = _outffn(xs, conv_s, od_s.astype(BF16), of_s.astype(BF16), wo, g2, wu, fwp, wd, gf, pf,
                           tm=rs, period=T_NEW, prompt=False, final=final, chunk_w=ffn_chunk)
        st_s.append((
            k_s.reshape(DB, T_NEW, n_da, 2, da_qk), v_s.reshape(DB, T_NEW, n_da, da_v),
            kf_s.reshape(DB, T_NEW, n_fox, fox_hd), vf_s.reshape(DB, T_NEW, n_fox, fox_hd), lf_s,
            gcu_s.reshape(DB, T_NEW, cw)[:, T_NEW - 2:], up_s.reshape(DB, T_NEW, 2 * dff)[:, T_NEW - 2:]))

    v_all, kt_all, kft_all, vft_all, lft_all = leaves_p
    prompt_leaves = [
        jnp.transpose(kt_all.reshape(depth, B, n_da, 2, da_qk, S), (0, 1, 5, 2, 3, 4)),
        v_all.reshape(depth, B, S, n_da, da_v),
        jnp.transpose(kft_all.reshape(depth, B, n_fox, fox_hd, S), (0, 1, 4, 2, 3)),
        jnp.transpose(vft_all.reshape(depth, B, n_fox, fox_hd, S), (0, 1, 4, 2, 3)),
        jnp.swapaxes(lft_all[:, :, :n_fox, :], 2, 3),
        jnp.stack([s[0] for s in st_p]), jnp.stack([s[1] for s in st_p])]
    outs = [xp, xs.reshape(DB, T_NEW, D)]
    for j in range(7):
        outs.append(prompt_leaves[j])
        outs.append(jnp.stack([s[j] for s in st_s]))
    return tuple(outs)
```

```python
import functools
import math

import jax
import jax.numpy as jnp
from jax import lax
from jax.experimental import pallas as pl
from jax.experimental.pallas import tpu as pltpu

F32, BF16, I32 = jnp.float32, jnp.bfloat16, jnp.int32
RMS_EPS = 1e-6
NEG_INF = -1e30
NUM_BUCKETS = 32
MAX_DISTANCE = 128
LANES = 128
SUBLANES = 8
VMEM_LIMIT_BYTES = 56 * 2**20
NT_DIMS = (((1,), (1,)), ((), ()))
LOG2E = math.log2(math.e)
PROMPT_TILE = 512
DECODE_PAGES = 32
FFN_CHUNK = 256


def _cparams(n_axes):
    return pltpu.CompilerParams(dimension_semantics=("arbitrary",) * n_axes, vmem_limit_bytes=VMEM_LIMIT_BYTES)


def _dot(a, b):
    return jnp.dot(a, b, preferred_element_type=F32)


def _dot_nt(a, b):
    return lax.dot_general(a, b, NT_DIMS, preferred_element_type=F32)


def _rms(x, g):
    return x * lax.rsqrt(jnp.mean(x * x, axis=-1, keepdims=True) + RMS_EPS) * g


def _log_sigmoid(x):
    return jnp.minimum(x, 0.0) - jnp.log1p(jnp.exp(-jnp.abs(x)))


def _conv3(u, p0, p1, w_ref, cols, period):
    rows = u.shape[0]
    t = lax.broadcasted_iota(I32, (rows, 1), 0)
    if period < rows:
        t = jnp.bitwise_and(t, period - 1)
    sh1 = jnp.where(t == 0, p1, pltpu.roll(u, 1, 0))
    sh2 = jnp.where(t == 0, p0, jnp.where(t == 1, p1, pltpu.roll(u, 2, 0)))
    return sh2 * w_ref[0:1, cols] + sh1 * w_ref[1:2, cols] + u * w_ref[2:3, cols]


def _prefix_sum_lanes(y, lane):
    for s in (1, 2, 4, 8, 16, 32, 64):
        y = y + jnp.where(lane >= s, pltpu.roll(y, s, 1), 0.0)
    return y


def _put_leaf(ref, prev_ref, val):
    if len(ref.shape) == 4:
        n_prev = ref.shape[0] - 1
        if n_prev:
            ref[0:n_prev, 0] = prev_ref[:, 0]
        ref[n_prev, 0] = val
    else:
        ref[0] = val


def _inproj_kernel(*refs, tm, period, prompt, stacked, da_scale, fox_scale, n_da, n_pairs):
    pv_ref = pkt_ref = pkft_ref = pvft_ref = plft_ref = None
    if prompt:
        if stacked:
            pv_ref, pkt_ref, pkft_ref, pvft_ref, plft_ref = refs[8:13]
            refs = refs[:8] + refs[13:]
        (x_ref, g_ref, wn_ref, wt_ref, wfl_ref, bf_ref, cw_ref, pref_ref,
         conv_ref, st_ref, qz_ref, v_ref, vb_ref, qfz_ref, kt_ref, kft_ref, vft_ref, lft_ref,
         ktb_ref, kftb_ref, vftb_ref, ct_ref, ccol_ref, carry_ref, ccarry_ref) = refs
    else:
        (x_ref, g_ref, wn_ref, wt_ref, wfl_ref, bf_ref, cw_ref, p0_ref, p1_ref,
         conv_ref, st_ref, qz_ref, v_ref, vb_ref, qfz_ref, kt_ref, kft_ref, vft_ref, lft_ref) = refs
    i = pl.program_id(1)
    cw = cw_ref.shape[1]
    dq = n_da * LANES
    dfx = n_pairs * LANES
    h = _rms(x_ref[0], g_ref[...]).astype(BF16)

    ugc = _dot(h, wn_ref[:, 0:3 * cw])
    gcu = ugc[:, 2 * cw:3 * cw] * ugc[:, 0:cw]
    if prompt:
        @pl.when(i == 0)
        def _():
            carry_ref[...] = pref_ref[0]
            ccarry_ref[...] = jnp.zeros_like(ccarry_ref)
        p0, p1 = carry_ref[6:7, :], carry_ref[7:8, :]
    else:
        p0, p1 = p0_ref[0], p1_ref[0]
    cy = _conv3(gcu, p0, p1, cw_ref, slice(None), period)
    conv_ref[0] = (ugc[:, cw:2 * cw] * cy).astype(BF16)
    if prompt:
        carry_ref[...] = gcu[tm - SUBLANES:tm]
        st_ref[0] = gcu[tm - SUBLANES:tm]
    else:
        st_ref[0] = gcu

    lo = lax.broadcasted_iota(I32, (tm, LANES), 1) < (LANES // 2)
    c0 = 3 * cw
    q = _dot(h, wn_ref[:, c0:c0 + dq]) * da_scale
    for hh in range(n_da):
        qh = q[:, hh * LANES:(hh + 1) * LANES]
        qz_ref[0, :, 2 * hh * LANES:(2 * hh + 1) * LANES] = jnp.where(lo, qh, 0.0).astype(BF16)
        qz_ref[0, :, (2 * hh + 1) * LANES:(2 * hh + 2) * LANES] = jnp.where(lo, 0.0, qh).astype(BF16)
    c0 += dq
    v = _dot(h, wn_ref[:, c0:c0 + dq])
    if prompt:
        n_prev = v_ref.shape[0] - 1
        if n_prev:
            v_ref[0:n_prev, 0] = pv_ref[:, 0]
        for hh in range(n_da):
            v_ref[n_prev, 0, pl.ds(hh, tm, stride=n_da), :] = v[:, hh * LANES:(hh + 1) * LANES]
    else:
        v_ref[0] = v
    vb_ref[0] = v.astype(BF16)
    c0 += dq
    qf = _dot(h, wn_ref[:, c0:c0 + dfx]) * fox_scale
    for p in range(n_pairs):
        qp = qf[:, p * LANES:(p + 1) * LANES]
        qfz_ref[0, :, 2 * p * LANES:(2 * p + 1) * LANES] = jnp.where(lo, qp, 0.0).astype(BF16)
        qfz_ref[0, :, (2 * p + 1) * LANES:(2 * p + 2) * LANES] = jnp.where(lo, 0.0, qp).astype(BF16)

    kt = _dot_nt(wt_ref[0:dq, :], h)
    _put_leaf(kt_ref, pkt_ref, kt)
    kft = _dot_nt(wt_ref[dq:dq + dfx, :], h)
    _put_leaf(kft_ref, pkft_ref, kft)
    vft = _dot_nt(wt_ref[dq + dfx:dq + 2 * dfx, :], h)
    _put_leaf(vft_ref, pvft_ref, vft)
    if prompt:
        ktb_ref[0, 0] = kt.astype(BF16)
        kftb_ref[0, 0] = kft.astype(BF16)
        vftb_ref[0, 0] = vft.astype(BF16)
    lf = _log_sigmoid(_dot_nt(wfl_ref[...], h)[0:SUBLANES] + bf_ref[...])
    _put_leaf(lft_ref, plft_ref, lf)

    if prompt:
        lane = lax.broadcasted_iota(I32, (SUBLANES, LANES), 1)
        carry = ccarry_ref[...]
        blocks = []
        for blk in range(tm // LANES):
            y = _prefix_sum_lanes(lf[:, blk * LANES:(blk + 1) * LANES], lane) + carry
            carry = jnp.broadcast_to(y[:, LANES - 1:LANES], (SUBLANES, LANES))
            blocks.append(y)
        ccarry_ref[...] = carry
        ct = jnp.concatenate(blocks, axis=1) * LOG2E
        ct_ref[0, 0] = ct
        ccol_ref[0] = jnp.concatenate([ct, jnp.zeros((LANES - SUBLANES, tm), F32)], axis=0).T


def _inproj(x, g1, wn, wt, wfl, bft, cwp, pre, *, tm, period, prompt, da_scale, fox_scale, prev=None):
    G, R, D = x.shape
    nt = R // tm
    cw = cwp.shape[1]
    dq = (wn.shape[1] - 3 * cw) * 2 // 5
    dfx = dq // 2
    n_da, n_pairs = dq // LANES, dfx // LANES
    row = lambda c: pl.BlockSpec((1, tm, c), lambda g, i: (g, i, 0))
    colT = lambda r: pl.BlockSpec((1, r, tm), lambda g, i: (g, 0, i))
    chunkT = lambda r: pl.BlockSpec((1, 1, r, tm), lambda g, i: (g, i, 0, 0))
    full = lambda a: pl.BlockSpec(a.shape, lambda g, i: (0,) * a.ndim)
    in_specs = [row(D), full(g1), full(wn), full(wt), full(wfl), full(bft), full(cwp)]
    if prompt:
        in_specs += [pl.BlockSpec((1, SUBLANES, cw), lambda g, i: (g, 0, 0))]
        ins = (x, g1, wn, wt, wfl, bft, cwp, pre)
        st_spec, st_shape = pl.BlockSpec((1, SUBLANES, cw), lambda g, i: (g, 0, 0)), (G, SUBLANES, cw)
        n_prev = prev[0].shape[0] if prev else 0
        rows_l = lambda n: pl.BlockSpec((n, 1, tm * n_da, LANES), lambda g, i: (0, g, i, 0))
        cols_l = lambda n, r: pl.BlockSpec((n, 1, r, tm), lambda g, i: (0, g, 0, i))
        if prev:
            in_specs += [rows_l(n_prev)] + [cols_l(n_prev, r) for r in (dq, dfx, dfx, SUBLANES)]
            ins += tuple(prev)
        L = n_prev + 1
        leaves = [(rows_l(L), jax.ShapeDtypeStruct((L, G, R * n_da, LANES), F32))]
        leaves += [(cols_l(L, r), jax.ShapeDtypeStruct((L, G, r, R), F32)) for r in (dq, dfx, dfx, SUBLANES)]
    else:
        in_specs += [row(cw), row(cw)]
        ins = (x, g1, wn, wt, wfl, bft, cwp, pre[0], pre[1])
        st_spec, st_shape = row(cw), (G, R, cw)
        leaves = [(row(dq), jax.ShapeDtypeStruct((G, R, dq), F32))]
        leaves += [(colT(r), jax.ShapeDtypeStruct((G, r, R), F32)) for r in (dq, dfx, dfx, SUBLANES)]
    outs = [
        (row(cw), jax.ShapeDtypeStruct((G, R, cw), BF16)),
        (st_spec, jax.ShapeDtypeStruct(st_shape, F32)),
        (row(2 * dq), jax.ShapeDtypeStruct((G, R, 2 * dq), BF16)),
        leaves[0],
        (row(dq), jax.ShapeDtypeStruct((G, R, dq), BF16)),
        (row(2 * dfx), jax.ShapeDtypeStruct((G, R, 2 * dfx), BF16)),
    ] + leaves[1:]
    scratch = []
    if prompt:
        outs += [(chunkT(r), jax.ShapeDtypeStruct((G, nt, r, tm), BF16)) for r in (dq, dfx, dfx)]
        outs += [
            (chunkT(SUBLANES), jax.ShapeDtypeStruct((G, nt, SUBLANES, tm), F32)),
            (row(LANES), jax.ShapeDtypeStruct((G, R, LANES), F32)),
        ]
        scratch = [pltpu.VMEM((SUBLANES, cw), F32), pltpu.VMEM((SUBLANES, LANES), F32)]
    kern = functools.partial(_inproj_kernel, tm=tm, period=period, prompt=prompt, stacked=bool(prev),
                             da_scale=da_scale, fox_scale=fox_scale, n_da=n_da, n_pairs=n_pairs)
    return pl.pallas_call(
        kern, grid=(G, nt), in_specs=in_specs, out_specs=[o[0] for o in outs],
        out_shape=[o[1] for o in outs], scratch_shapes=scratch, compiler_params=_cparams(2),
        name="inproj_prompt" if prompt else "inproj_sample")(*ins)


def _bias_of_distance(n, tab_ref, h, n_heads):
    max_exact = NUM_BUCKETS // 2
    nf = jnp.maximum(n, 1).astype(F32)
    large = max_exact + (jnp.log(nf / max_exact) / math.log(MAX_DISTANCE / max_exact)
                         * (NUM_BUCKETS - max_exact)).astype(I32)
    bucket = jnp.where(n < max_exact, n, jnp.minimum(large, NUM_BUCKETS - 1))
    far = tab_ref[(NUM_BUCKETS - 1) * n_heads + h]
    out = jnp.zeros(n.shape, F32)
    for b in range(NUM_BUCKETS - 1):
        out = jnp.where(bucket == b, (tab_ref[b * n_heads + h] - far) * LOG2E, out)
    return out


def _bias_kernel(tab_ref, d_ref, bp_ref, bn_ref, *, t, past, n_heads, new_pad, dec_seq):
    r = lax.broadcasted_iota(I32, (t, t), 0)
    c = lax.broadcasted_iota(I32, (t, t), 1)
    rp = lax.broadcasted_iota(I32, (SUBLANES, past), 0)
    kp = lax.broadcasted_iota(I32, (SUBLANES, past), 1)
    rn = lax.broadcasted_iota(I32, (SUBLANES, new_pad), 0)
    kn = lax.broadcasted_iota(I32, (SUBLANES, new_pad), 1)
    tp = jnp.bitwise_and(rp, dec_seq - 1)
    tn = jnp.bitwise_and(rn, dec_seq - 1)
    for h in range(n_heads):
        d_ref[h, 0] = jnp.where(r >= c, _bias_of_distance(jnp.maximum(r - c, 0), tab_ref, h, n_heads), NEG_INF)
        d_ref[h, 1] = _bias_of_distance(t + r - c, tab_ref, h, n_heads)
        bp_ref[h * SUBLANES:(h + 1) * SUBLANES, :] = _bias_of_distance(past + tp - kp, tab_ref, h, n_heads)
        bn_ref[h * SUBLANES:(h + 1) * SUBLANES, :] = jnp.where(
            kn <= tn, _bias_of_distance(jnp.maximum(tn - kn, 0), tab_ref, h, n_heads), NEG_INF)


def _bias_tiles(rel_bias, *, t, past, new_pad, dec_seq):
    n_heads = rel_bias.shape[1]
    kern = functools.partial(_bias_kernel, t=t, past=past, n_heads=n_heads, new_pad=new_pad, dec_seq=dec_seq)
    return pl.pallas_call(
        kern,
        in_specs=[pl.BlockSpec(memory_space=pltpu.SMEM)],
        out_specs=[pl.BlockSpec(memory_space=pltpu.VMEM)] * 3,
        out_shape=[jax.ShapeDtypeStruct((n_heads, 2, t, t), F32),
                   jax.ShapeDtypeStruct((n_heads * SUBLANES, past), F32),
                   jax.ShapeDtypeStruct((n_heads * SUBLANES, new_pad), F32)],
        compiler_params=pltpu.CompilerParams(vmem_limit_bytes=VMEM_LIMIT_BYTES),
        name="rel_bias_tiles")(rel_bias.reshape(-1))


def _lambda_of(lp, lam_init):
    a = jnp.sum(lp[0:1] * lp[1:2], axis=-1, keepdims=True)
    b = jnp.sum(lp[2:3] * lp[3:4], axis=-1, keepdims=True)
    return jnp.exp(a) - jnp.exp(b) + lam_init


def _flash_update(s, m_ref, acc_ref, pv, row_off=None):
    blocks = [s[:, c * LANES:(c + 1) * LANES] for c in range(s.shape[1] // LANES)]
    rmax = jnp.max(functools.reduce(jnp.maximum, blocks), axis=-1, keepdims=True)
    m_old = m_ref[...]
    m_new = jnp.maximum(m_old, rmax if row_off is None else rmax + row_off)
    alpha = jnp.exp2(m_old - m_new)
    shift = m_new if row_off is None else m_new - row_off
    p = jnp.concatenate([jnp.exp2(b - shift).astype(BF16) for b in blocks], axis=1)
    acc_ref[...] = jnp.concatenate([alpha, alpha], axis=1) * acc_ref[...] + pv(p)
    m_ref[...] = m_new


CHUNKS_PER_BLOCK = 8


def _run_chunks(n, chunk, last):
    def group(jg, carry):
        for u in range(CHUNKS_PER_BLOCK):
            chunk(CHUNKS_PER_BLOCK * jg + u)
        return carry

    n_groups = n // CHUNKS_PER_BLOCK
    lax.fori_loop(0, n_groups, group, 0)
    base = n_groups * CHUNKS_PER_BLOCK
    for rem in range(CHUNKS_PER_BLOCK):
        @pl.when(n - base == rem)
        def _():
            for u in range(rem):
                chunk(base + u)
            last()


def _diff_prompt_kernel(q_ref, k_ref, v_ref, d_ref, lam_ref, g_ref, o_ref, m1, a1, m2, a2, *, t, lam_init):
    i = pl.program_id(2)
    for m, a in ((m1, a1), (m2, a2)):
        m[...] = jnp.full_like(m, NEG_INF)
        a[...] = jnp.zeros_like(a)
    q1, q2 = q_ref[0, :, 0:LANES], q_ref[0, :, LANES:2 * LANES]

    corner = d_ref[0, 1, 0:LANES, t - LANES:t]

    def chunk(j, diagonal=False):
        kt = k_ref[0, j]
        v = v_ref[0, pl.ds(pl.multiple_of(j * t, t), t), :]
        v1 = jnp.concatenate([v, jnp.ones_like(v)], axis=1)
        for q, m, a in ((q1, m1, a1), (q2, m2, a2)):
            s = _dot(q, kt)
            if diagonal:
                s = s + d_ref[0, 0]
            else:
                near = s[0:LANES, t - LANES:] + jnp.where(j == i - 1, corner, 0.0)
                if t > LANES:
                    top = jnp.concatenate([s[0:LANES, :t - LANES], near], axis=1)
                    s = jnp.concatenate([top, s[LANES:]], axis=0)
                else:
                    s = near
            _flash_update(s, m, a, lambda p: _dot(p, v1))

    _run_chunks(i, chunk, lambda: chunk(i, diagonal=True))

    lam = _lambda_of(lam_ref[...], lam_init)
    o = a1[:, 0:LANES] / a1[:, LANES:] - lam * (a2[:, 0:LANES] / a2[:, LANES:])
    o_ref[0] = (_rms(o, g_ref[...]) * (1.0 - lam_init)).astype(BF16)


def _diff_prompt(qz, ktb, vb, dtiles, lam_p, subln, *, t, lam_init):
    B, S, _ = qz.shape
    H = dtiles.shape[0]
    nq = S // t
    kern = functools.partial(_diff_prompt_kernel, t=t, lam_init=lam_init)
    return pl.pallas_call(
        kern, grid=(B, H, nq),
        in_specs=[pl.BlockSpec((1, t, 2 * LANES), lambda b, h, i: (b, i, h)),
                  pl.BlockSpec((1, nq, LANES, t), lambda b, h, i: (b, 0, h, 0)),
                  pl.BlockSpec((1, S, LANES), lambda b, h, i: (b, 0, h)),
                  pl.BlockSpec((1, 2, t, t), lambda b, h, i: (h, 0, 0, 0)),
                  pl.BlockSpec(lam_p.shape, lambda b, h, i: (0, 0)),
                  pl.BlockSpec(subln.shape, lambda b, h, i: (0, 0))],
        out_specs=pl.BlockSpec((1, t, LANES), lambda b, h, i: (b, i, h)),
        out_shape=jax.ShapeDtypeStruct((B, S, H * LANES), BF16),
        scratch_shapes=[pltpu.VMEM((t, LANES), F32), pltpu.VMEM((t, 2 * LANES), F32)] * 2,
        compiler_params=_cparams(3), name="diff_attn_prompt")(qz, ktb, vb, dtiles, lam_p, subln)


def _fox_prompt_kernel(q_ref, k_ref, v_ref, c_ref, cc_ref, o_ref, m0, a0, m1, a1, *, t):
    pr = pl.program_id(1)
    i = pl.program_id(2)
    state = ((m0, a0), (m1, a1))
    for m, a in state:
        m[...] = jnp.full_like(m, NEG_INF)
        a[...] = jnp.zeros_like(a)
    lane = lax.broadcasted_iota(I32, (t, LANES), 1)
    ccol = cc_ref[0]
    heads = []
    for e in range(2):
        hsel = lane == (2 * pr + e)
        cq = jnp.sum(jnp.where(hsel, ccol, 0.0), axis=-1, keepdims=True)
        cbase = cq[0:1, :]
        heads.append((q_ref[0, :, e * LANES:(e + 1) * LANES], jnp.broadcast_to(cq - cbase, (t, LANES)), cbase, state[e]))

    def chunk(j, diag):
        kt = k_ref[0, j]
        vt = v_ref[0, j]
        vt1 = jnp.concatenate([vt, jnp.ones_like(vt)], axis=0)
        crow = c_ref[0, j]
        crow_idx = lax.broadcasted_iota(I32, crow.shape, 0)
        for e, (q, cqr, cbase, (m, a)) in enumerate(heads):
            ck = jnp.sum(jnp.where(crow_idx == 2 * pr + e, crow, 0.0), axis=0, keepdims=True)
            z = _dot(q, kt) - (ck - cbase)
            if diag:
                r = lax.broadcasted_iota(I32, (t, t), 0)
                c = lax.broadcasted_iota(I32, (t, t), 1)
                z = jnp.where(r >= c, z, NEG_INF)
            _flash_update(z, m, a, lambda p: _dot_nt(p, vt1), row_off=cqr)

    _run_chunks(i, lambda j: chunk(j, False), lambda: chunk(i, True))
    o_ref[0] = jnp.where(lane < LANES // 2, a0[:, 0:LANES] / a0[:, LANES:], a1[:, 0:LANES] / a1[:, LANES:]).astype(BF16)


def _fox_prompt(qfz, kftb, vftb, ct, ccol, *, t):
    B, S, _ = qfz.shape
    nq = S // t
    n_pairs = kftb.shape[2] // LANES
    kern = functools.partial(_fox_prompt_kernel, t=t)
    return pl.pallas_call(
        kern, grid=(B, n_pairs, nq),
        in_specs=[pl.BlockSpec((1, t, 2 * LANES), lambda b, p, i: (b, i, p)),
                  pl.BlockSpec((1, nq, LANES, t), lambda b, p, i: (b, 0, p, 0)),
                  pl.BlockSpec((1, nq, LANES, t), lambda b, p, i: (b, 0, p, 0)),
                  pl.BlockSpec((1, nq, SUBLANES, t), lambda b, p, i: (b, 0, 0, 0)),
                  pl.BlockSpec((1, t, LANES), lambda b, p, i: (b, i, 0))],
        out_specs=pl.BlockSpec((1, t, LANES), lambda b, p, i: (b, i, p)),
        out_shape=jax.ShapeDtypeStruct((B, S, n_pairs * LANES), BF16),
        scratch_shapes=[pltpu.VMEM((t, LANES), F32), pltpu.VMEM((t, 2 * LANES), F32)] * 2,
        compiler_params=_cparams(3), name="fox_attn_prompt")(qfz, kftb, vftb, ct, ccol)


def _outffn_kernel(*refs, tm, period, prompt, final, chunk_w):
    if prompt:
        (x_ref, cv_ref, od_ref, of_ref, wo_ref, g2_ref, wu_ref, fw_ref, wd_ref, gf_ref, pref_ref,
         xo_ref, st_ref, act_ref, carry_ref) = refs
    else:
        (x_ref, cv_ref, od_ref, of_ref, wo_ref, g2_ref, wu_ref, fw_ref, wd_ref, gf_ref, p0_ref, p1_ref,
         xo_ref, st_ref, act_ref) = refs
    i = pl.program_id(1)
    cw, dw = cv_ref.shape[2], od_ref.shape[2]
    dff = wd_ref.shape[0]
    x1 = (x_ref[0] + _dot(cv_ref[0], wo_ref[0:cw, :]) + _dot(od_ref[0], wo_ref[cw:cw + dw, :])
          + _dot(of_ref[0], wo_ref[cw + dw:, :]))
    xn = _rms(x1, g2_ref[...]).astype(BF16)
    if prompt:
        @pl.when(i == 0)
        def _():
            carry_ref[...] = pref_ref[0]
    for c in range(dff // chunk_w):
        halves = []
        for base in (c * chunk_w, dff + c * chunk_w):
            cols = slice(base, base + chunk_w)
            up = _dot(xn, wu_ref[:, cols])
            if prompt:
                p0, p1 = carry_ref[6:7, cols], carry_ref[7:8, cols]
            else:
                p0, p1 = p0_ref[0, :, cols], p1_ref[0, :, cols]
            halves.append(_conv3(up, p0, p1, fw_ref, cols, period))
            if prompt:
                carry_ref[:, cols] = up[tm - SUBLANES:tm]
                st_ref[0, :, cols] = up[tm - SUBLANES:tm]
            else:
                st_ref[0, :, cols] = up
        val, gate = halves
        act = gate * (1.0 / (1.0 + jnp.exp(-gate))) * val
        act_ref[:, c * chunk_w:(c + 1) * chunk_w] = act.astype(BF16)
    x2 = x1 + _dot(act_ref[...], wd_ref[...])
    xo_ref[0] = _rms(x2, gf_ref[...]) if final else x2


def _outffn(x, conv, od, of, wo, g2, wu, fw, wd, gf, pre, *, tm, period, prompt, final, chunk_w):
    G, R, D = x.shape
    nt = R // tm
    dff2 = wu.shape[1]
    row = lambda c: pl.BlockSpec((1, tm, c), lambda g, i: (g, i, 0))
    once = lambda a: pl.BlockSpec(a.shape, lambda g, i: (0,) * a.ndim, pipeline_mode=pl.Buffered(1))
    in_specs = [row(D), row(conv.shape[2]), row(od.shape[2]), row(of.shape[2]),
                once(wo), once(g2), once(wu), once(fw), once(wd), once(gf)]
    if prompt:
        in_specs += [pl.BlockSpec((1, SUBLANES, dff2), lambda g, i: (g, 0, 0))]
        ins = (x, conv, od, of, wo, g2, wu, fw, wd, gf, pre)
        st_spec, st_shape = pl.BlockSpec((1, SUBLANES, dff2), lambda g, i: (g, 0, 0)), (G, SUBLANES, dff2)
        scratch = [pltpu.VMEM((tm, dff2 // 2), BF16), pltpu.VMEM((SUBLANES, dff2), F32)]
    else:
        in_specs += [row(dff2), row(dff2)]
        ins = (x, conv, od, of, wo, g2, wu, fw, wd, gf, pre[0], pre[1])
        st_spec, st_shape = row(dff2), (G, R, dff2)
        scratch = [pltpu.VMEM((tm, dff2 // 2), BF16)]
    kern = functools.partial(_outffn_kernel, tm=tm, period=period, prompt=prompt, final=final, chunk_w=chunk_w)
    return pl.pallas_call(
        kern, grid=(G, nt), in_specs=in_specs, out_specs=[row(D), st_spec],
        out_shape=[jax.ShapeDtypeStruct((G, R, D), F32), jax.ShapeDtypeStruct(st_shape, F32)],
        scratch_shapes=scratch, compiler_params=_cparams(2),
        name="outffn_prompt" if prompt else "outffn_sample")(*ins)


def _diff_decode_kernel(pt_ref, q_ref, *refs, pages, n_heads, lam_init, dec_seq):
    kp = refs[0:pages]
    vp = refs[pages:2 * pages]
    bp_ref, kn_ref, vn_ref, bn_ref, lam_ref, g_ref, o_ref, m_ref, l_ref, a_ref = refs[2 * pages:]
    c = pl.program_id(1)
    q = q_ref[0]
    rows = q.shape[0]

    def update(s, pv):
        m_old = m_ref[...]
        m_new = jnp.maximum(m_old, jnp.max(s, axis=-1, keepdims=True))
        alpha = jnp.exp2(m_old - m_new)
        p = jnp.exp2(s - m_new)
        l_ref[...] = alpha * l_ref[...] + jnp.sum(p, axis=-1, keepdims=True)
        a_ref[...] = alpha * a_ref[...] + pv(p.astype(BF16))
        m_ref[...] = m_new

    def per_head_pv(p, v_of_head):
        outs = []
        for h in range(n_heads):
            g = (h // 2) * 2 * SUBLANES
            res = _dot(p[g:g + 2 * SUBLANES, :], v_of_head(h))
            outs.append(res[(h % 2) * SUBLANES:(h % 2 + 1) * SUBLANES, :])
        return jnp.concatenate(outs, axis=0)

    @pl.when(c == 0)
    def _():
        m_ref[...] = jnp.full_like(m_ref, NEG_INF)
        l_ref[...] = jnp.zeros_like(l_ref)
        a_ref[...] = jnp.zeros_like(a_ref)
        vn = vn_ref[0]
        update(_dot_nt(q, kn_ref[0]) + bn_ref[...],
               lambda p: per_head_pv(p, lambda h: vn[:, h * LANES:(h + 1) * LANES]))

    kt = jnp.concatenate([r[0, 0] for r in kp], axis=1).astype(BF16)
    s = _dot(q, kt) + bp_ref[...]

    def v_of_head(h):
        return jnp.concatenate([r[0, 0, pl.ds(h, LANES, stride=n_heads), :] for r in vp], axis=0).astype(BF16)

    update(s, lambda p: per_head_pv(p, v_of_head))

    @pl.when(c == pl.num_programs(1) - 1)
    def _():
        lam = _lambda_of(lam_ref[...], lam_init)
        on = a_ref[...] / l_ref[...]
        o = on - lam * pltpu.roll(on, rows - dec_seq, 0)
        o_ref[0] = _rms(o, g_ref[...]) * (1.0 - lam_init)


def _fox_decode_kernel(pt_ref, q_ref, *refs, pages, n_heads, dec_seq):
    kp = refs[0:pages]
    vp = refs[pages:2 * pages]
    fp = refs[2 * pages:3 * pages]
    kn_ref, vn_ref, ln_ref, o_ref, m_ref, l_ref, a_ref, cq_ref, run_ref = refs[3 * pages:]
    c = pl.program_id(1)
    q = q_ref[0]
    rows = q.shape[0]

    def update(z, vt):
        m_old = m_ref[...]
        m_new = jnp.maximum(m_old, jnp.max(z, axis=-1, keepdims=True))
        alpha = jnp.exp2(m_old - m_new)
        p = jnp.exp2(z - m_new)
        l_ref[...] = alpha * l_ref[...] + jnp.sum(p, axis=-1, keepdims=True)
        a_ref[...] = alpha * a_ref[...] + _dot_nt(p.astype(BF16), vt)
        m_ref[...] = m_new

    @pl.when(c == 0)
    def _():
        m_ref[...] = jnp.full_like(m_ref, NEG_INF)
        l_ref[...] = jnp.zeros_like(l_ref)
        a_ref[...] = jnp.zeros_like(a_ref)
        run_ref[...] = jnp.zeros_like(run_ref)
        lane = lax.broadcasted_iota(I32, (rows, LANES), 1)
        trow = jnp.bitwise_and(lax.broadcasted_iota(I32, (rows, LANES), 0), dec_seq - 1)
        cnew = ln_ref[0] * LOG2E
        for s_ in (1, 2):
            cnew = cnew + jnp.where(lane >= s_, pltpu.roll(cnew, s_, 1), 0.0)
        cq = jnp.sum(jnp.where(lane == trow, cnew, 0.0), axis=-1, keepdims=True)
        cq_ref[...] = cq
        z = jnp.where(lane <= trow, _dot_nt(q, kn_ref[0]) + cq - cnew, NEG_INF)
        update(z, vn_ref[0])

    lane8 = lax.broadcasted_iota(I32, (SUBLANES, LANES), 1)
    later = run_ref[...]
    r_pages = [None] * pages
    for jj in reversed(range(pages)):
        y = _prefix_sum_lanes(fp[jj][0, 0] * LOG2E, lane8)
        tot = jnp.broadcast_to(y[:, LANES - 1:LANES], (SUBLANES, LANES))
        r_pages[jj] = tot - y + later
        later = later + tot
    run_ref[...] = later
    r8 = jnp.concatenate(r_pages, axis=1)
    row_head = lax.shift_right_logical(lax.broadcasted_iota(I32, (rows, 1), 0), dec_seq.bit_length() - 1)
    r16 = jnp.zeros((rows, r8.shape[1]), F32)
    for h in range(n_heads):
        r16 = jnp.where(row_head == h, r8[h:h + 1, :], r16)
    kt = jnp.concatenate([r[0, 0] for r in kp], axis=1).astype(BF16)
    vt = jnp.concatenate([r[0, 0] for r in vp], axis=1).astype(BF16)
    update(_dot(q, kt) + r16 + cq_ref[...], vt)

    @pl.when(c == pl.num_programs(1) - 1)
    def _():
        o_ref[0] = a_ref[...] / l_ref[...]


def _decode_kernel(pt_ref, *refs, pages, n_da, n_fox, lam_init, dec_seq):
    n_d, n_f = 1 + 2 * pages + 6, 1 + 3 * pages + 3
    d_in, f_in = refs[:n_d], refs[n_d:n_d + n_f]
    od_ref, of_ref = refs[n_d + n_f:n_d + n_f + 2]
    sc = refs[n_d + n_f + 2:]
    _diff_decode_kernel(pt_ref, *d_in, od_ref, *sc[0:3], pages=pages, n_heads=n_da, lam_init=lam_init, dec_seq=dec_seq)
    _fox_decode_kernel(pt_ref, *f_in, of_ref, *sc[3:8], pages=pages, n_heads=n_fox, dec_seq=dec_seq)


def _decode(pt_flat, layer, qbd, dkt, dv4, bias_past, knew, vnew, bias_new, lam_p, subln,
            qfbd, fkt, fvt, flt, kfnew, vfnew_t, lnew, *, pages, n_pages, lam_init, dec_seq):
    DB, drows, dq = qbd.shape
    _, frows, dfx = qfbd.shape
    nc = n_pages // pages
    per_seq = lambda a: pl.BlockSpec((1,) + a.shape[1:], lambda b, c, pt: (b, 0, 0))
    shared = lambda a: pl.BlockSpec(a.shape, lambda b, c, pt: (0, 0))

    def page_spec(jj, r, newest_first):
        chunk = (lambda c: nc - 1 - c) if newest_first else (lambda c: c)
        return pl.BlockSpec((1, 1, r, LANES),
                            lambda b, c, pt: (layer, pt[b * n_pages + chunk(c) * pages + jj], 0, 0))

    in_specs = ([per_seq(qbd)]
                + [page_spec(jj, dq, False) for jj in range(pages)] * 2
                + [pl.BlockSpec((drows, pages * LANES), lambda b, c, pt: (0, c)),
                   per_seq(knew), per_seq(vnew), shared(bias_new), shared(lam_p), shared(subln)]
                + [per_seq(qfbd)]
                + [page_spec(jj, dfx, True) for jj in range(pages)] * 2
                + [page_spec(jj, SUBLANES, True) for jj in range(pages)]
                + [per_seq(kfnew), per_seq(vfnew_t), per_seq(lnew)])
    kern = functools.partial(_decode_kernel, pages=pages, n_da=dq // LANES, n_fox=frows // dec_seq,
                             lam_init=lam_init, dec_seq=dec_seq)
    return pl.pallas_call(
        kern,
        grid_spec=pltpu.PrefetchScalarGridSpec(
            num_scalar_prefetch=1, grid=(DB, nc), in_specs=in_specs,
            out_specs=[pl.BlockSpec((1, drows, LANES), lambda b, c, pt: (b, 0, 0)),
                       pl.BlockSpec((1, frows, dfx), lambda b, c, pt: (b, 0, 0))],
            scratch_shapes=[pltpu.VMEM((drows, 1), F32), pltpu.VMEM((drows, 1), F32), pltpu.VMEM((drows, LANES), F32),
                            pltpu.VMEM((frows, 1), F32), pltpu.VMEM((frows, 1), F32), pltpu.VMEM((frows, dfx), F32),
                            pltpu.VMEM((frows, 1), F32), pltpu.VMEM((SUBLANES, LANES), F32)]),
        out_shape=[jax.ShapeDtypeStruct((DB, drows, LANES), F32), jax.ShapeDtypeStruct((DB, frows, dfx), F32)],
        compiler_params=_cparams(2), name="attn_decode",
    )(pt_flat, qbd, *([dkt] * pages), *([dv4] * pages), bias_past, knew, vnew, bias_new, lam_p, subln,
      qfbd, *([fkt] * pages), *([fvt] * pages), *([flt] * pages), kfnew, vfnew_t, lnew)


def _pad_rows(a, rows):
    return jnp.pad(a, ((0, rows - a.shape[0]),) + ((0, 0),) * (a.ndim - 1))


def _tile(n, pref):
    return pref if n % pref == 0 else n


def kernel(x_prompt, x_sample, cache_dk, cache_dv, cache_fk, cache_fv, cache_flogf, state_conv, state_ffn, page_table, rel_bias, norm1, w_in, b_f, conv_w, diff_lambda, subln, w_out, norm2, w_up, ffn_conv, w_down, norm_f):
    B, S, D = x_prompt.shape
    DB, T_NEW, _ = x_sample.shape
    depth, n_phys, page, n_da, _, da_qk = cache_dk.shape
    da_v = cache_dv.shape[-1]
    n_fox, fox_hd = cache_fk.shape[3], cache_fk.shape[4]
    cw = conv_w.shape[-1]
    dff = w_down.shape[1]
    n_pages = page_table.shape[1]
    past = n_pages * page
    dq, dfx = n_da * 2 * da_qk, n_fox * fox_hd
    assert page == LANES and 2 * da_qk == LANES and da_v == LANES and 2 * fox_hd == LANES
    assert n_fox % 2 == 0 and T_NEW & (T_NEW - 1) == 0 and T_NEW <= SUBLANES // 2
    assert (DB * T_NEW) % LANES == 0 and S % LANES == 0
    da_scale, fox_scale = da_qk ** -0.5, fox_hd ** -0.5
    t = _tile(S, PROMPT_TILE)
    rs = DB * T_NEW
    pages = _tile(n_pages, DECODE_PAGES)
    new_pad = 2 * SUBLANES
    ffn_chunk = _tile(dff, FFN_CHUNK)

    dkt = jnp.transpose(cache_dk, (0, 1, 3, 4, 5, 2)).reshape(depth, n_phys, dq, page)
    dv4 = cache_dv.reshape(depth, n_phys, page * n_da, da_v)
    fkt = jnp.transpose(cache_fk, (0, 1, 3, 4, 2)).reshape(depth, n_phys, dfx, page)
    fvt = jnp.transpose(cache_fv, (0, 1, 3, 4, 2)).reshape(depth, n_phys, dfx, page)
    flt = jnp.pad(jnp.swapaxes(cache_flogf, 2, 3), ((0, 0), (0, 0), (0, SUBLANES - n_fox), (0, 0)))
    pt_flat = page_table.reshape(-1)

    dtiles, bias_past, bias_new = _bias_tiles(rel_bias, t=t, past=past, new_pad=new_pad, dec_seq=T_NEW)

    xp = x_prompt
    xs = x_sample.reshape(1, rs, D)
    zeros_c = jnp.zeros((B, SUBLANES, cw), F32)
    zeros_f = jnp.zeros((B, SUBLANES, 2 * dff), F32)
    eye_da = jnp.eye(n_da, dtype=BF16)
    pair_of_head = (jnp.arange(n_fox)[:, None] // 2 == jnp.arange(n_fox // 2)[None, :]).astype(BF16)
    st_p, st_s, leaves_p = [], [], None
    for l in range(depth):
        lam_init = 0.8 - 0.6 * math.exp(-0.3 * l)
        final = l == depth - 1
        wl = w_in[l]
        s0, s1 = 3 * cw, 3 * cw + dq
        wn = jnp.concatenate([wl[:, :s1], wl[:, s1 + dq:s1 + 2 * dq], wl[:, s1 + 2 * dq:s1 + 2 * dq + dfx]],
                             axis=1).astype(BF16)
        wt = jnp.concatenate([wl[:, s1:s1 + dq], wl[:, s1 + 2 * dq + dfx:s1 + 2 * dq + 3 * dfx]], axis=1).T.astype(BF16)
        wfl = _pad_rows(wl[:, s1 + 2 * dq + 3 * dfx:].T, 2 * SUBLANES).astype(BF16)
        bft = _pad_rows(b_f[l][:, None], SUBLANES)
        cwp = _pad_rows(conv_w[l], SUBLANES)
        fwp = _pad_rows(ffn_conv[l], SUBLANES)
        g1, g2, gf, gs = norm1[l][None], norm2[l][None], norm_f[None], subln[l][None]
        wo, wu, wd = w_out[l].astype(BF16), w_up[l].astype(BF16), w_down[l].astype(BF16)
        lam_p = diff_lambda[l]
        inproj = functools.partial(_inproj, da_scale=da_scale * LOG2E, fox_scale=fox_scale * LOG2E)

        (conv, cst, qz, v, vb, qfz, kt, kft, vft, lft, ktb, kftb, vftb, ct, ccol) = inproj(
            xp, g1, wn, wt, wfl, bft, cwp, zeros_c, tm=t, period=t, prompt=True, prev=leaves_p)
        leaves_p = (v, kt, kft, vft, lft)
        od = _diff_prompt(qz, ktb, vb, dtiles, lam_p, gs, t=t, lam_init=lam_init)
        of = _fox_prompt(qfz, kftb, vftb, ct, ccol, t=t)
        xp, fst = _outffn(xp, conv, od, of, wo, g2, wu, fwp, wd, gf, zeros_f,
                          tm=t, period=t, prompt=True, final=final, chunk_w=ffn_chunk)
        st_p.append((cst[:, SUBLANES - 2:, :], fst[:, SUBLANES - 2:, :]))

        pc = (jnp.repeat(state_conv[l][:, 0], T_NEW, axis=0)[None], jnp.repeat(state_conv[l][:, 1], T_NEW, axis=0)[None])
        pf = (jnp.repeat(state_ffn[l][:, 0], T_NEW, axis=0)[None], jnp.repeat(state_ffn[l][:, 1], T_NEW, axis=0)[None])
        (conv_s, gcu_s, qz_s, v_s, vb_s, qfz_s, kt_s, kft_s, vft_s, lft_s) = inproj(
            xs, g1, wn, wt, wfl, bft, cwp, pc, tm=rs, period=T_NEW, prompt=False)
        k_s = kt_s[0].T.reshape(DB, T_NEW, dq)
        kf_s = kft_s[0].T.reshape(DB, T_NEW, dfx)
        vf_s = vft_s[0].T.reshape(DB, T_NEW, dfx)
        lf_s = lft_s[0, :n_fox].T.reshape(DB, T_NEW, n_fox)
        pad_new = lambda a: jnp.pad(a, ((0, 0), (0, new_pad - T_NEW), (0, 0)))
        pad_lane = lambda a: jnp.pad(a, ((0, 0), (0, LANES - T_NEW), (0, 0)))
        q5 = jnp.transpose(qz_s.reshape(DB, T_NEW, n_da, 2, LANES), (0, 2, 3, 1, 4))
        qbd = (q5[:, :, :, :, None, :] * eye_da[None, :, None, None, :, None]).reshape(DB, n_da * 2 * T_NEW, dq)
        qf4 = jnp.transpose(qfz_s.reshape(DB, T_NEW, n_fox, LANES), (0, 2, 1, 3))
        qfbd = (qf4[:, :, :, None, :] * pair_of_head[None, :, None, :, None]).reshape(DB, n_fox * T_NEW, dfx)
        lnew = jnp.pad(jnp.broadcast_to(jnp.swapaxes(lf_s, 1, 2)[:, :, None, :], (DB, n_fox, T_NEW, T_NEW)),
                       ((0, 0), (0, 0), (0, 0), (0, LANES - T_NEW))).reshape(DB, n_fox * T_NEW, LANES)
        od_s, of_s = _decode(pt_flat, l, qbd, dkt, dv4, bias_past, pad_new(k_s).astype(BF16),
                             pad_new(vb_s.reshape(DB, T_NEW, dq)), bias_new, lam_p, gs,
                             qfbd, fkt, fvt, flt, pad_lane(kf_s).astype(BF16),
                             jnp.swapaxes(pad_lane(vf_s), 1, 2).astype(BF16), lnew,
                             pages=pages, n_pages=n_pages, lam_init=lam_init, dec_seq=T_NEW)
        od_s = jnp.transpose(od_s.reshape(DB, n_da, 2, T_NEW, da_v)[:, :, 0], (0, 2, 1, 3)).reshape(1, rs, dq)
        of5 = of_s.reshape(DB, n_fox, T_NEW, n_fox, fox_hd)
        of_s = jnp.transpose(jnp.einsum('bhthd->bhtd', of5), (0, 2, 1, 3)).reshape(1, rs, dfx)
        xs, up_s = _outffn(xs, conv_s, od_s.astype(BF16), of_s.astype(BF16), wo, g2, wu, fwp, wd, gf, pf,
                           tm=rs, period=T_NEW, prompt=False, final=final, chunk_w=ffn_chunk)
        st_s.append((
            k_s.reshape(DB, T_NEW, n_da, 2, da_qk), v_s.reshape(DB, T_NEW, n_da, da_v),
            kf_s.reshape(DB, T_NEW, n_fox, fox_hd), vf_s.reshape(DB, T_NEW, n_fox, fox_hd), lf_s,
            gcu_s.reshape(DB, T_NEW, cw)[:, T_NEW - 2:], up_s.reshape(DB, T_NEW, 2 * dff)[:, T_NEW - 2:]))

    v_all, kt_all, kft_all, vft_all, lft_all = leaves_p
    prompt_leaves = [
        jnp.transpose(kt_all.reshape(depth, B, n_da, 2, da_qk, S), (0, 1, 5, 2, 3, 4)),
        v_all.reshape(depth, B, S, n_da, da_v),
        jnp.transpose(kft_all.reshape(depth, B, n_fox, fox_hd, S), (0, 1, 4, 2, 3)),
        jnp.transpose(vft_all.reshape(depth, B, n_fox, fox_hd, S), (0, 1, 4, 2, 3)),
        jnp.swapaxes(lft_all[:, :, :n_fox, :], 2, 3),
        jnp.stack([s[0] for s in st_p]), jnp.stack([s[1] for s in st_p])]
    outs = [xp, xs.reshape(DB, T_NEW, D)]
    for j in range(7):
        outs.append(prompt_leaves[j])
        outs.append(jnp.stack([s[j] for s in st_s]))
    return tuple(outs)
```

```python
import functools
import math

import jax
import jax.numpy as jnp
from jax import lax
from jax.experimental import pallas as pl
from jax.experimental.pallas import tpu as pltpu

F32, BF16, I32 = jnp.float32, jnp.bfloat16, jnp.int32
RMS_EPS = 1e-6
NEG_INF = -1e30
NUM_BUCKETS = 32
MAX_DISTANCE = 128
LANES = 128
SUBLANES = 8
VMEM_LIMIT_BYTES = 56 * 2**20
NT_DIMS = (((1,), (1,)), ((), ()))
LOG2E = math.log2(math.e)
PROMPT_TILE = 512
DECODE_PAGES = 32
FFN_CHUNK = 256


def _cparams(n_axes):
    return pltpu.CompilerParams(dimension_semantics=("arbitrary",) * n_axes, vmem_limit_bytes=VMEM_LIMIT_BYTES)


def _dot(a, b):
    return jnp.dot(a, b, preferred_element_type=F32)


def _dot_nt(a, b):
    return lax.dot_general(a, b, NT_DIMS, preferred_element_type=F32)


def _rms(x, g):
    return x * lax.rsqrt(jnp.mean(x * x, axis=-1, keepdims=True) + RMS_EPS) * g


def _log_sigmoid(x):
    return jnp.minimum(x, 0.0) - jnp.log1p(jnp.exp(-jnp.abs(x)))


def _conv3(u, p0, p1, w_ref, cols, period):
    rows = u.shape[0]
    t = lax.broadcasted_iota(I32, (rows, 1), 0)
    if period < rows:
        t = jnp.bitwise_and(t, period - 1)
    sh1 = jnp.where(t == 0, p1, pltpu.roll(u, 1, 0))
    sh2 = jnp.where(t == 0, p0, jnp.where(t == 1, p1, pltpu.roll(u, 2, 0)))
    return sh2 * w_ref[0:1, cols] + sh1 * w_ref[1:2, cols] + u * w_ref[2:3, cols]


def _prefix_sum_lanes(y, lane):
    for s in (1, 2, 4, 8, 16, 32, 64):
        y = y + jnp.where(lane >= s, pltpu.roll(y, s, 1), 0.0)
    return y


def _put_leaf(ref, prev_ref, val):
    if len(ref.shape) == 4:
        n_prev = ref.shape[0] - 1
        if n_prev:
            ref[0:n_prev, 0] = prev_ref[:, 0]
        ref[n_prev, 0] = val
    else:
        ref[0] = val


def _inproj_kernel(*refs, tm, period, prompt, stacked, da_scale, fox_scale, n_da, n_pairs):
    pv_ref = pkt_ref = pkft_ref = pvft_ref = plft_ref = None
    if prompt:
        if stacked:
            pv_ref, pkt_ref, pkft_ref, pvft_ref, plft_ref = refs[8:13]
            refs = refs[:8] + refs[13:]
        (x_ref, g_ref, wn_ref, wt_ref, wfl_ref, bf_ref, cw_ref, pref_ref,
         conv_ref, st_ref, qz_ref, v_ref, vb_ref, qfz_ref, kt_ref, kft_ref, vft_ref, lft_ref,
         ktb_ref, kftb_ref, vftb_ref, ct_ref, ccol_ref, carry_ref, ccarry_ref) = refs
    else:
        (x_ref, g_ref, wn_ref, wt_ref, wfl_ref, bf_ref, cw_ref, p0_ref, p1_ref,
         conv_ref, st_ref, qz_ref, v_ref, vb_ref, qfz_ref, kt_ref, kft_ref, vft_ref, lft_ref) = refs
    i = pl.program_id(1)
    cw = cw_ref.shape[1]
    dq = n_da * LANES
    dfx = n_pairs * LANES
    h = _rms(x_ref[0], g_ref[...]).astype(BF16)

    ugc = _dot(h, wn_ref[:, 0:3 * cw])
    gcu = ugc[:, 2 * cw:3 * cw] * ugc[:, 0:cw]
    if prompt:
        @pl.when(i == 0)
        def _():
            carry_ref[...] = pref_ref[0]
            ccarry_ref[...] = jnp.zeros_like(ccarry_ref)
        p0, p1 = carry_ref[6:7, :], carry_ref[7:8, :]
    else:
        p0, p1 = p0_ref[0], p1_ref[0]
    cy = _conv3(gcu, p0, p1, cw_ref, slice(None), period)
    conv_ref[0] = (ugc[:, cw:2 * cw] * cy).astype(BF16)
    if prompt:
        carry_ref[...] = gcu[tm - SUBLANES:tm]
        st_ref[0] = gcu[tm - SUBLANES:tm]
    else:
        st_ref[0] = gcu

    lo = lax.broadcasted_iota(I32, (tm, LANES), 1) < (LANES // 2)
    c0 = 3 * cw
    q = _dot(h, wn_ref[:, c0:c0 + dq]) * da_scale
    for hh in range(n_da):
        qh = q[:, hh * LANES:(hh + 1) * LANES]
        qz_ref[0, :, 2 * hh * LANES:(2 * hh + 1) * LANES] = jnp.where(lo, qh, 0.0).astype(BF16)
        qz_ref[0, :, (2 * hh + 1) * LANES:(2 * hh + 2) * LANES] = jnp.where(lo, 0.0, qh).astype(BF16)
    c0 += dq
    v = _dot(h, wn_ref[:, c0:c0 + dq])
    if prompt:
        n_prev = v_ref.shape[0] - 1
        if n_prev:
            v_ref[0:n_prev, 0] = pv_ref[:, 0]
        for hh in range(n_da):
            v_ref[n_prev, 0, pl.ds(hh, tm, stride=n_da), :] = v[:, hh * LANES:(hh + 1) * LANES]
    else:
        v_ref[0] = v
    vb_ref[0] = v.astype(BF16)
    c0 += dq
    qf = _dot(h, wn_ref[:, c0:c0 + dfx]) * fox_scale
    for p in range(n_pairs):
        qp = qf[:, p * LANES:(p + 1) * LANES]
        qfz_ref[0, :, 2 * p * LANES:(2 * p + 1) * LANES] = jnp.where(lo, qp, 0.0).astype(BF16)
        qfz_ref[0, :, (2 * p + 1) * LANES:(2 * p + 2) * LANES] = jnp.where(lo, 0.0, qp).astype(BF16)

    kt = _dot_nt(wt_ref[0:dq, :], h)
    _put_leaf(kt_ref, pkt_ref, kt)
    kft = _dot_nt(wt_ref[dq:dq + dfx, :], h)
    _put_leaf(kft_ref, pkft_ref, kft)
    vft = _dot_nt(wt_ref[dq + dfx:dq + 2 * dfx, :], h)
    _put_leaf(vft_ref, pvft_ref, vft)
    if prompt:
        ktb_ref[0, 0] = kt.astype(BF16)
        kftb_ref[0, 0] = kft.astype(BF16)
        vftb_ref[0, 0] = vft.astype(BF16)
    lf = _log_sigmoid(_dot_nt(wfl_ref[...], h)[0:SUBLANES] + bf_ref[...])
    _put_leaf(lft_ref, plft_ref, lf)

    if prompt:
        lane = lax.broadcasted_iota(I32, (SUBLANES, LANES), 1)
        carry = ccarry_ref[...]
        blocks = []
        for blk in range(tm // LANES):
            y = _prefix_sum_lanes(lf[:, blk * LANES:(blk + 1) * LANES], lane) + carry
            carry = jnp.broadcast_to(y[:, LANES - 1:LANES], (SUBLANES, LANES))
            blocks.append(y)
        ccarry_ref[...] = carry
        ct = jnp.concatenate(blocks, axis=1) * LOG2E
        ct_ref[0, 0] = ct
        ccol_ref[0] = jnp.concatenate([ct, jnp.zeros((LANES - SUBLANES, tm), F32)], axis=0).T


def _inproj(x, g1, wn, wt, wfl, bft, cwp, pre, *, tm, period, prompt, da_scale, fox_scale, prev=None):
    G, R, D = x.shape
    nt = R // tm
    cw = cwp.shape[1]
    dq = (wn.shape[1] - 3 * cw) * 2 // 5
    dfx = dq // 2
    n_da, n_pairs = dq // LANES, dfx // LANES
    row = lambda c: pl.BlockSpec((1, tm, c), lambda g, i: (g, i, 0))
    colT = lambda r: pl.BlockSpec((1, r, tm), lambda g, i: (g, 0, i))
    chunkT = lambda r: pl.BlockSpec((1, 1, r, tm), lambda g, i: (g, i, 0, 0))
    full = lambda a: pl.BlockSpec(a.shape, lambda g, i: (0,) * a.ndim)
    in_specs = [row(D), full(g1), full(wn), full(wt), full(wfl), full(bft), full(cwp)]
    if prompt:
        in_specs += [pl.BlockSpec((1, SUBLANES, cw), lambda g, i: (g, 0, 0))]
        ins = (x, g1, wn, wt, wfl, bft, cwp, pre)
        st_spec, st_shape = pl.BlockSpec((1, SUBLANES, cw), lambda g, i: (g, 0, 0)), (G, SUBLANES, cw)
        n_prev = prev[0].shape[0] if prev else 0
        rows_l = lambda n: pl.BlockSpec((n, 1, tm * n_da, LANES), lambda g, i: (0, g, i, 0))
        cols_l = lambda n, r: pl.BlockSpec((n, 1, r, tm), lambda g, i: (0, g, 0, i))
        if prev:
            in_specs += [rows_l(n_prev)] + [cols_l(n_prev, r) for r in (dq, dfx, dfx, SUBLANES)]
            ins += tuple(prev)
        L = n_prev + 1
        leaves = [(rows_l(L), jax.ShapeDtypeStruct((L, G, R * n_da, LANES), F32))]
        leaves += [(cols_l(L, r), jax.ShapeDtypeStruct((L, G, r, R), F32)) for r in (dq, dfx, dfx, SUBLANES)]
    else:
        in_specs += [row(cw), row(cw)]
        ins = (x, g1, wn, wt, wfl, bft, cwp, pre[0], pre[1])
        st_spec, st_shape = row(cw), (G, R, cw)
        leaves = [(row(dq), jax.ShapeDtypeStruct((G, R, dq), F32))]
        leaves += [(colT(r), jax.ShapeDtypeStruct((G, r, R), F32)) for r in (dq, dfx, dfx, SUBLANES)]
    outs = [
        (row(cw), jax.ShapeDtypeStruct((G, R, cw), BF16)),
        (st_spec, jax.ShapeDtypeStruct(st_shape, F32)),
        (row(2 * dq), jax.ShapeDtypeStruct((G, R, 2 * dq), BF16)),
        leaves[0],
        (row(dq), jax.ShapeDtypeStruct((G, R, dq), BF16)),
        (row(2 * dfx), jax.ShapeDtypeStruct((G, R, 2 * dfx), BF16)),
    ] + leaves[1:]
    scratch = []
    if prompt:
        outs += [(chunkT(r), jax.ShapeDtypeStruct((G, nt, r, tm), BF16)) for r in (dq, dfx, dfx)]
        outs += [
            (chunkT(SUBLANES), jax.ShapeDtypeStruct((G, nt, SUBLANES, tm), F32)),
            (row(LANES), jax.ShapeDtypeStruct((G, R, LANES), F32)),
        ]
        scratch = [pltpu.VMEM((SUBLANES, cw), F32), pltpu.VMEM((SUBLANES, LANES), F32)]
    kern = functools.partial(_inproj_kernel, tm=tm, period=period, prompt=prompt, stacked=bool(prev),
                             da_scale=da_scale, fox_scale=fox_scale, n_da=n_da, n_pairs=n_pairs)
    return pl.pallas_call(
        kern, grid=(G, nt), in_specs=in_specs, out_specs=[o[0] for o in outs],
        out_shape=[o[1] for o in outs], scratch_shapes=scratch, compiler_params=_cparams(2),
        name="inproj_prompt" if prompt else "inproj_sample")(*ins)


def _bias_of_distance(n, tab_ref, h, n_heads):
    max_exact = NUM_BUCKETS // 2
    nf = jnp.maximum(n, 1).astype(F32)
    large = max_exact + (jnp.log(nf / max_exact) / math.log(MAX_DISTANCE / max_exact)
                         * (NUM_BUCKETS - max_exact)).astype(I32)
    bucket = jnp.where(n < max_exact, n, jnp.minimum(large, NUM_BUCKETS - 1))
    far = tab_ref[(NUM_BUCKETS - 1) * n_heads + h]
    out = jnp.zeros(n.shape, F32)
    for b in range(NUM_BUCKETS - 1):
        out = jnp.where(bucket == b, (tab_ref[b * n_heads + h] - far) * LOG2E, out)
    return out


def _bias_kernel(tab_ref, d_ref, bp_ref, bn_ref, *, t, past, n_heads, new_pad, dec_seq):
    r = lax.broadcasted_iota(I32, (t, t), 0)
    c = lax.broadcasted_iota(I32, (t, t), 1)
    rp = lax.broadcasted_iota(I32, (SUBLANES, past), 0)
    kp = lax.broadcasted_iota(I32, (SUBLANES, past), 1)
    rn = lax.broadcasted_iota(I32, (SUBLANES, new_pad), 0)
    kn = lax.broadcasted_iota(I32, (SUBLANES, new_pad), 1)
    tp = jnp.bitwise_and(rp, dec_seq - 1)
    tn = jnp.bitwise_and(rn, dec_seq - 1)
    for h in range(n_heads):
        d_ref[h, 0] = jnp.where(r >= c, _bias_of_distance(jnp.maximum(r - c, 0), tab_ref, h, n_heads), NEG_INF)
        d_ref[h, 1] = _bias_of_distance(t + r - c, tab_ref, h, n_heads)
        bp_ref[h * SUBLANES:(h + 1) * SUBLANES, :] = _bias_of_distance(past + tp - kp, tab_ref, h, n_heads)
        bn_ref[h * SUBLANES:(h + 1) * SUBLANES, :] = jnp.where(
            kn <= tn, _bias_of_distance(jnp.maximum(tn - kn, 0), tab_ref, h, n_heads), NEG_INF)


def _bias_tiles(rel_bias, *, t, past, new_pad, dec_seq):
    n_heads = rel_bias.shape[1]
    kern = functools.partial(_bias_kernel, t=t, past=past, n_heads=n_heads, new_pad=new_pad, dec_seq=dec_seq)
    return pl.pallas_call(
        kern,
        in_specs=[pl.BlockSpec(memory_space=pltpu.SMEM)],
        out_specs=[pl.BlockSpec(memory_space=pltpu.VMEM)] * 3,
        out_shape=[jax.ShapeDtypeStruct((n_heads, 2, t, t), F32),
                   jax.ShapeDtypeStruct((n_heads * SUBLANES, past), F32),
                   jax.ShapeDtypeStruct((n_heads * SUBLANES, new_pad), F32)],
        compiler_params=pltpu.CompilerParams(vmem_limit_bytes=VMEM_LIMIT_BYTES),
        name="rel_bias_tiles")(rel_bias.reshape(-1))


def _lambda_of(lp, lam_init):
    a = jnp.sum(lp[0:1] * lp[1:2], axis=-1, keepdims=True)
    b = jnp.sum(lp[2:3] * lp[3:4], axis=-1, keepdims=True)
    return jnp.exp(a) - jnp.exp(b) + lam_init


def _flash_update(s, m_ref, acc_ref, pv, row_off=None):
    blocks = [s[:, c * LANES:(c + 1) * LANES] for c in range(s.shape[1] // LANES)]
    rmax = jnp.max(functools.reduce(jnp.maximum, blocks), axis=-1, keepdims=True)
    m_old = m_ref[...]
    m_new = jnp.maximum(m_old, rmax if row_off is None else rmax + row_off)
    alpha = jnp.exp2(m_old - m_new)
    shift = m_new if row_off is None else m_new - row_off
    p = jnp.concatenate([jnp.exp2(b - shift).astype(BF16) for b in blocks], axis=1)
    acc_ref[...] = jnp.concatenate([alpha, alpha], axis=1) * acc_ref[...] + pv(p)
    m_ref[...] = m_new


CHUNKS_PER_BLOCK = 8


def _run_chunks(n, chunk, last):
    def group(jg, carry):
        for u in range(CHUNKS_PER_BLOCK):
            chunk(CHUNKS_PER_BLOCK * jg + u)
        return carry

    n_groups = n // CHUNKS_PER_BLOCK
    lax.fori_loop(0, n_groups, group, 0)
    base = n_groups * CHUNKS_PER_BLOCK
    for rem in range(CHUNKS_PER_BLOCK):
        @pl.when(n - base == rem)
        def _():
            for u in range(rem):
                chunk(base + u)
            last()


def _diff_prompt_kernel(q_ref, k_ref, v_ref, d_ref, lam_ref, g_ref, o_ref, m1, a1, m2, a2, *, t, lam_init):
    i = pl.program_id(2)
    for m, a in ((m1, a1), (m2, a2)):
        m[...] = jnp.full_like(m, NEG_INF)
        a[...] = jnp.zeros_like(a)
    q1, q2 = q_ref[0, :, 0:LANES], q_ref[0, :, LANES:2 * LANES]

    corner = d_ref[0, 1, 0:LANES, t - LANES:t]

    def chunk(j, diagonal=False):
        kt = k_ref[0, j]
        v = v_ref[0, pl.ds(pl.multiple_of(j * t, t), t), :]
        v1 = jnp.concatenate([v, jnp.ones_like(v)], axis=1)
        for q, m, a in ((q1, m1, a1), (q2, m2, a2)):
            s = _dot(q, kt)
            if diagonal:
                s = s + d_ref[0, 0]
            else:
                near = s[0:LANES, t - LANES:] + jnp.where(j == i - 1, corner, 0.0)
                if t > LANES:
                    top = jnp.concatenate([s[0:LANES, :t - LANES], near], axis=1)
                    s = jnp.concatenate([top, s[LANES:]], axis=0)
                else:
                    s = near
            _flash_update(s, m, a, lambda p: _dot(p, v1))

    _run_chunks(i, chunk, lambda: chunk(i, diagonal=True))

    lam = _lambda_of(lam_ref[...], lam_init)
    o = a1[:, 0:LANES] / a1[:, LANES:] - lam * (a2[:, 0:LANES] / a2[:, LANES:])
    o_ref[0] = (_rms(o, g_ref[...]) * (1.0 - lam_init)).astype(BF16)


def _diff_prompt(qz, ktb, vb, dtiles, lam_p, subln, *, t, lam_init):
    B, S, _ = qz.shape
    H = dtiles.shape[0]
    nq = S // t
    kern = functools.partial(_diff_prompt_kernel, t=t, lam_init=lam_init)
    return pl.pallas_call(
        kern, grid=(B, H, nq),
        in_specs=[pl.BlockSpec((1, t, 2 * LANES), lambda b, h, i: (b, i, h)),
                  pl.BlockSpec((1, nq, LANES, t), lambda b, h, i: (b, 0, h, 0)),
                  pl.BlockSpec((1, S, LANES), lambda b, h, i: (b, 0, h)),
                  pl.BlockSpec((1, 2, t, t), lambda b, h, i: (h, 0, 0, 0)),
                  pl.BlockSpec(lam_p.shape, lambda b, h, i: (0, 0)),
                  pl.BlockSpec(subln.shape, lambda b, h, i: (0, 0))],
        out_specs=pl.BlockSpec((1, t, LANES), lambda b, h, i: (b, i, h)),
        out_shape=jax.ShapeDtypeStruct((B, S, H * LANES), BF16),
        scratch_shapes=[pltpu.VMEM((t, LANES), F32), pltpu.VMEM((t, 2 * LANES), F32)] * 2,
        compiler_params=_cparams(3), name="diff_attn_prompt")(qz, ktb, vb, dtiles, lam_p, subln)


def _fox_prompt_kernel(q_ref, k_ref, v_ref, c_ref, cc_ref, o_ref, m0, a0, m1, a1, *, t):
    pr = pl.program_id(1)
    i = pl.program_id(2)
    state = ((m0, a0), (m1, a1))
    for m, a in state:
        m[...] = jnp.full_like(m, NEG_INF)
        a[...] = jnp.zeros_like(a)
    lane = lax.broadcasted_iota(I32, (t, LANES), 1)
    ccol = cc_ref[0]
    heads = []
    for e in range(2):
        hsel = lane == (2 * pr + e)
        cq = jnp.sum(jnp.where(hsel, ccol, 0.0), axis=-1, keepdims=True)
        cbase = cq[0:1, :]
        heads.append((q_ref[0, :, e * LANES:(e + 1) * LANES], jnp.broadcast_to(cq - cbase, (t, LANES)), cbase, state[e]))

    def chunk(j, diag):
        kt = k_ref[0, j]
        vt = v_ref[0, j]
        vt1 = jnp.concatenate([vt, jnp.ones_like(vt)], axis=0)
        crow = c_ref[0, j]
        crow_idx = lax.broadcasted_iota(I32, crow.shape, 0)
        for e, (q, cqr, cbase, (m, a)) in enumerate(heads):
            ck = jnp.sum(jnp.where(crow_idx == 2 * pr + e, crow, 0.0), axis=0, keepdims=True)
            z = _dot(q, kt) - (ck - cbase)
            if diag:
                r = lax.broadcasted_iota(I32, (t, t), 0)
                c = lax.broadcasted_iota(I32, (t, t), 1)
                z = jnp.where(r >= c, z, NEG_INF)
            _flash_update(z, m, a, lambda p: _dot_nt(p, vt1), row_off=cqr)

    _run_chunks(i, lambda j: chunk(j, False), lambda: chunk(i, True))
    o_ref[0] = jnp.where(lane < LANES // 2, a0[:, 0:LANES] / a0[:, LANES:], a1[:, 0:LANES] / a1[:, LANES:]).astype(BF16)


def _fox_prompt(qfz, kftb, vftb, ct, ccol, *, t):
    B, S, _ = qfz.shape
    nq = S // t
    n_pairs = kftb.shape[2] // LANES
    kern = functools.partial(_fox_prompt_kernel, t=t)
    return pl.pallas_call(
        kern, grid=(B, n_pairs, nq),
        in_specs=[pl.BlockSpec((1, t, 2 * LANES), lambda b, p, i: (b, i, p)),
                  pl.BlockSpec((1, nq, LANES, t), lambda b, p, i: (b, 0, p, 0)),
                  pl.BlockSpec((1, nq, LANES, t), lambda b, p, i: (b, 0, p, 0)),
                  pl.BlockSpec((1, nq, SUBLANES, t), lambda b, p, i: (b, 0, 0, 0)),
                  pl.BlockSpec((1, t, LANES), lambda b, p, i: (b, i, 0))],
        out_specs=pl.BlockSpec((1, t, LANES), lambda b, p, i: (b, i, p)),
        out_shape=jax.ShapeDtypeStruct((B, S, n_pairs * LANES), BF16),
        scratch_shapes=[pltpu.VMEM((t, LANES), F32), pltpu.VMEM((t, 2 * LANES), F32)] * 2,
        compiler_params=_cparams(3), name="fox_attn_prompt")(qfz, kftb, vftb, ct, ccol)


def _outffn_kernel(*refs, tm, period, prompt, final, chunk_w):
    if prompt:
        (x_ref, cv_ref, od_ref, of_ref, wo_ref, g2_ref, wu_ref, fw_ref, wd_ref, gf_ref, pref_ref,
         xo_ref, st_ref, act_ref, carry_ref) = refs
    else:
        (x_ref, cv_ref, od_ref, of_ref, wo_ref, g2_ref, wu_ref, fw_ref, wd_ref, gf_ref, p0_ref, p1_ref,
         xo_ref, st_ref, act_ref) = refs
    i = pl.program_id(1)
    cw, dw = cv_ref.shape[2], od_ref.shape[2]
    dff = wd_ref.shape[1]
    x1 = (x_ref[0] + _dot(cv_ref[0], wo_ref[0, 0:cw, :]) + _dot(od_ref[0], wo_ref[0, cw:cw + dw, :])
          + _dot(of_ref[0], wo_ref[0, cw + dw:, :]))
    xn = _rms(x1, g2_ref[...]).astype(BF16)
    if prompt:
        @pl.when(i == 0)
        def _():
            carry_ref[...] = pref_ref[0]
    for c in range(dff // chunk_w):
        halves = []
        for base in (c * chunk_w, dff + c * chunk_w):
            cols = slice(base, base + chunk_w)
            up = _dot(xn, wu_ref[0, :, cols])
            if prompt:
                p0, p1 = carry_ref[6:7, cols], carry_ref[7:8, cols]
            else:
                p0, p1 = p0_ref[0, :, cols], p1_ref[0, :, cols]
            halves.append(_conv3(up, p0, p1, fw_ref, cols, period))
            if prompt:
                carry_ref[:, cols] = up[tm - SUBLANES:tm]
                st_ref[0, :, cols] = up[tm - SUBLANES:tm]
            else:
                st_ref[0, :, cols] = up
        val, gate = halves
        act = gate * (1.0 / (1.0 + jnp.exp(-gate))) * val
        act_ref[:, c * chunk_w:(c + 1) * chunk_w] = act.astype(BF16)
    x2 = x1 + _dot(act_ref[...], wd_ref[0])
    xo_ref[0] = _rms(x2, gf_ref[...]) if final else x2


def _outffn(x, conv, od, of, wo, g2, wu, fw, wd, gf, pre, *, layer, tm, period, prompt, final, chunk_w):
    G, R, D = x.shape
    nt = R // tm
    dff2 = wu.shape[2]
    row = lambda c: pl.BlockSpec((1, tm, c), lambda g, i: (g, i, 0))
    once = lambda a: pl.BlockSpec(a.shape, lambda g, i: (0,) * a.ndim, pipeline_mode=pl.Buffered(1))
    of_layer = lambda a: pl.BlockSpec((1,) + a.shape[1:], lambda g, i: (layer, 0, 0), pipeline_mode=pl.Buffered(1))
    in_specs = [row(D), row(conv.shape[2]), row(od.shape[2]), row(of.shape[2]),
                of_layer(wo), once(g2), of_layer(wu), once(fw), of_layer(wd), once(gf)]
    if prompt:
        in_specs += [pl.BlockSpec((1, SUBLANES, dff2), lambda g, i: (g, 0, 0))]
        ins = (x, conv, od, of, wo, g2, wu, fw, wd, gf, pre)
        st_spec, st_shape = pl.BlockSpec((1, SUBLANES, dff2), lambda g, i: (g, 0, 0)), (G, SUBLANES, dff2)
        scratch = [pltpu.VMEM((tm, dff2 // 2), BF16), pltpu.VMEM((SUBLANES, dff2), F32)]
    else:
        in_specs += [row(dff2), row(dff2)]
        ins = (x, conv, od, of, wo, g2, wu, fw, wd, gf, pre[0], pre[1])
        st_spec, st_shape = row(dff2), (G, R, dff2)
        scratch = [pltpu.VMEM((tm, dff2 // 2), BF16)]
    kern = functools.partial(_outffn_kernel, tm=tm, period=period, prompt=prompt, final=final, chunk_w=chunk_w)
    return pl.pallas_call(
        kern, grid=(G, nt), in_specs=in_specs, out_specs=[row(D), st_spec],
        out_shape=[jax.ShapeDtypeStruct((G, R, D), F32), jax.ShapeDtypeStruct(st_shape, F32)],
        scratch_shapes=scratch, compiler_params=_cparams(2),
        name="outffn_prompt" if prompt else "outffn_sample")(*ins)


def _diff_decode_kernel(pt_ref, q_ref, *refs, pages, n_heads, lam_init, dec_seq):
    kp = refs[0:pages]
    vp = refs[pages:2 * pages]
    bp_ref, kn_ref, vn_ref, bn_ref, lam_ref, g_ref, o_ref, m_ref, l_ref, a_ref = refs[2 * pages:]
    c = pl.program_id(1)
    q = q_ref[0]
    rows = q.shape[0]

    def update(s, pv):
        m_old = m_ref[...]
        m_new = jnp.maximum(m_old, jnp.max(s, axis=-1, keepdims=True))
        alpha = jnp.exp2(m_old - m_new)
        p = jnp.exp2(s - m_new)
        l_ref[...] = alpha * l_ref[...] + jnp.sum(p, axis=-1, keepdims=True)
        a_ref[...] = alpha * a_ref[...] + pv(p.astype(BF16))
        m_ref[...] = m_new

    def per_head_pv(p, v_of_head):
        outs = []
        for h in range(n_heads):
            g = (h // 2) * 2 * SUBLANES
            res = _dot(p[g:g + 2 * SUBLANES, :], v_of_head(h))
            outs.append(res[(h % 2) * SUBLANES:(h % 2 + 1) * SUBLANES, :])
        return jnp.concatenate(outs, axis=0)

    @pl.when(c == 0)
    def _():
        m_ref[...] = jnp.full_like(m_ref, NEG_INF)
        l_ref[...] = jnp.zeros_like(l_ref)
        a_ref[...] = jnp.zeros_like(a_ref)
        vn = vn_ref[0]
        update(_dot_nt(q, kn_ref[0]) + bn_ref[...],
               lambda p: per_head_pv(p, lambda h: vn[:, h * LANES:(h + 1) * LANES]))

    kt = jnp.concatenate([r[0, 0] for r in kp], axis=1).astype(BF16)
    s = _dot(q, kt) + bp_ref[...]

    def v_of_head(h):
        return jnp.concatenate([r[0, 0, pl.ds(h, LANES, stride=n_heads), :] for r in vp], axis=0).astype(BF16)

    update(s, lambda p: per_head_pv(p, v_of_head))

    @pl.when(c == pl.num_programs(1) - 1)
    def _():
        lam = _lambda_of(lam_ref[...], lam_init)
        on = a_ref[...] / l_ref[...]
        o = on - lam * pltpu.roll(on, rows - dec_seq, 0)
        o_ref[0] = _rms(o, g_ref[...]) * (1.0 - lam_init)


def _fox_decode_kernel(pt_ref, q_ref, *refs, pages, n_heads, dec_seq):
    kp = refs[0:pages]
    vp = refs[pages:2 * pages]
    fp = refs[2 * pages:3 * pages]
    kn_ref, vn_ref, ln_ref, o_ref, m_ref, l_ref, a_ref, cq_ref, run_ref = refs[3 * pages:]
    c = pl.program_id(1)
    q = q_ref[0]
    rows = q.shape[0]

    def update(z, vt):
        m_old = m_ref[...]
        m_new = jnp.maximum(m_old, jnp.max(z, axis=-1, keepdims=True))
        alpha = jnp.exp2(m_old - m_new)
        p = jnp.exp2(z - m_new)
        l_ref[...] = alpha * l_ref[...] + jnp.sum(p, axis=-1, keepdims=True)
        a_ref[...] = alpha * a_ref[...] + _dot_nt(p.astype(BF16), vt)
        m_ref[...] = m_new

    @pl.when(c == 0)
    def _():
        m_ref[...] = jnp.full_like(m_ref, NEG_INF)
        l_ref[...] = jnp.zeros_like(l_ref)
        a_ref[...] = jnp.zeros_like(a_ref)
        run_ref[...] = jnp.zeros_like(run_ref)
        lane = lax.broadcasted_iota(I32, (rows, LANES), 1)
        trow = jnp.bitwise_and(lax.broadcasted_iota(I32, (rows, LANES), 0), dec_seq - 1)
        cnew = ln_ref[0] * LOG2E
        for s_ in (1, 2):
            cnew = cnew + jnp.where(lane >= s_, pltpu.roll(cnew, s_, 1), 0.0)
        cq = jnp.sum(jnp.where(lane == trow, cnew, 0.0), axis=-1, keepdims=True)
        cq_ref[...] = cq
        z = jnp.where(lane <= trow, _dot_nt(q, kn_ref[0]) + cq - cnew, NEG_INF)
        update(z, vn_ref[0])

    lane8 = lax.broadcasted_iota(I32, (SUBLANES, LANES), 1)
    later = run_ref[...]
    r_pages = [None] * pages
    for jj in reversed(range(pages)):
        y = _prefix_sum_lanes(fp[jj][0, 0] * LOG2E, lane8)
        tot = jnp.broadcast_to(y[:, LANES - 1:LANES], (SUBLANES, LANES))
        r_pages[jj] = tot - y + later
        later = later + tot
    run_ref[...] = later
    r8 = jnp.concatenate(r_pages, axis=1)
    row_head = lax.shift_right_logical(lax.broadcasted_iota(I32, (rows, 1), 0), dec_seq.bit_length() - 1)
    r16 = jnp.zeros((rows, r8.shape[1]), F32)
    for h in range(n_heads):
        r16 = jnp.where(row_head == h, r8[h:h + 1, :], r16)
    kt = jnp.concatenate([r[0, 0] for r in kp], axis=1).astype(BF16)
    vt = jnp.concatenate([r[0, 0] for r in vp], axis=1).astype(BF16)
    update(_dot(q, kt) + r16 + cq_ref[...], vt)

    @pl.when(c == pl.num_programs(1) - 1)
    def _():
        o_ref[0] = a_ref[...] / l_ref[...]


def _decode_kernel(pt_ref, *refs, pages, n_da, n_fox, lam_init, dec_seq):
    n_d, n_f = 1 + 2 * pages + 6, 1 + 3 * pages + 3
    d_in, f_in = refs[:n_d], refs[n_d:n_d + n_f]
    od_ref, of_ref = refs[n_d + n_f:n_d + n_f + 2]
    sc = refs[n_d + n_f + 2:]
    _diff_decode_kernel(pt_ref, *d_in, od_ref, *sc[0:3], pages=pages, n_heads=n_da, lam_init=lam_init, dec_seq=dec_seq)
    _fox_decode_kernel(pt_ref, *f_in, of_ref, *sc[3:8], pages=pages, n_heads=n_fox, dec_seq=dec_seq)


def _decode(pt_flat, layer, qbd, dkt, dv4, bias_past, knew, vnew, bias_new, lam_p, subln,
            qfbd, fkt, fvt, flt, kfnew, vfnew_t, lnew, *, pages, n_pages, lam_init, dec_seq):
    DB, drows, dq = qbd.shape
    _, frows, dfx = qfbd.shape
    nc = n_pages // pages
    per_seq = lambda a: pl.BlockSpec((1,) + a.shape[1:], lambda b, c, pt: (b, 0, 0))
    shared = lambda a: pl.BlockSpec(a.shape, lambda b, c, pt: (0, 0))

    def page_spec(jj, r, newest_first):
        chunk = (lambda c: nc - 1 - c) if newest_first else (lambda c: c)
        return pl.BlockSpec((1, 1, r, LANES),
                            lambda b, c, pt: (layer, pt[b * n_pages + chunk(c) * pages + jj], 0, 0))

    in_specs = ([per_seq(qbd)]
                + [page_spec(jj, dq, False) for jj in range(pages)] * 2
                + [pl.BlockSpec((drows, pages * LANES), lambda b, c, pt: (0, c)),
                   per_seq(knew), per_seq(vnew), shared(bias_new), shared(lam_p), shared(subln)]
                + [per_seq(qfbd)]
                + [page_spec(jj, dfx, True) for jj in range(pages)] * 2
                + [page_spec(jj, SUBLANES, True) for jj in range(pages)]
                + [per_seq(kfnew), per_seq(vfnew_t), per_seq(lnew)])
    kern = functools.partial(_decode_kernel, pages=pages, n_da=dq // LANES, n_fox=frows // dec_seq,
                             lam_init=lam_init, dec_seq=dec_seq)
    return pl.pallas_call(
        kern,
        grid_spec=pltpu.PrefetchScalarGridSpec(
            num_scalar_prefetch=1, grid=(DB, nc), in_specs=in_specs,
            out_specs=[pl.BlockSpec((1, drows, LANES), lambda b, c, pt: (b, 0, 0)),
                       pl.BlockSpec((1, frows, dfx), lambda b, c, pt: (b, 0, 0))],
            scratch_shapes=[pltpu.VMEM((drows, 1), F32), pltpu.VMEM((drows, 1), F32), pltpu.VMEM((drows, LANES), F32),
                            pltpu.VMEM((frows, 1), F32), pltpu.VMEM((frows, 1), F32), pltpu.VMEM((frows, dfx), F32),
                            pltpu.VMEM((frows, 1), F32), pltpu.VMEM((SUBLANES, LANES), F32)]),
        out_shape=[jax.ShapeDtypeStruct((DB, drows, LANES), F32), jax.ShapeDtypeStruct((DB, frows, dfx), F32)],
        compiler_params=_cparams(2), name="attn_decode",
    )(pt_flat, qbd, *([dkt] * pages), *([dv4] * pages), bias_past, knew, vnew, bias_new, lam_p, subln,
      qfbd, *([fkt] * pages), *([fvt] * pages), *([flt] * pages), kfnew, vfnew_t, lnew)


def _pad_rows(a, rows):
    return jnp.pad(a, ((0, rows - a.shape[0]),) + ((0, 0),) * (a.ndim - 1))


def _tile(n, pref):
    return pref if n % pref == 0 else n


def kernel(x_prompt, x_sample, cache_dk, cache_dv, cache_fk, cache_fv, cache_flogf, state_conv, state_ffn, page_table, rel_bias, norm1, w_in, b_f, conv_w, diff_lambda, subln, w_out, norm2, w_up, ffn_conv, w_down, norm_f):
    B, S, D = x_prompt.shape
    DB, T_NEW, _ = x_sample.shape
    depth, n_phys, page, n_da, _, da_qk = cache_dk.shape
    da_v = cache_dv.shape[-1]
    n_fox, fox_hd = cache_fk.shape[3], cache_fk.shape[4]
    cw = conv_w.shape[-1]
    dff = w_down.shape[1]
    n_pages = page_table.shape[1]
    past = n_pages * page
    dq, dfx = n_da * 2 * da_qk, n_fox * fox_hd
    assert page == LANES and 2 * da_qk == LANES and da_v == LANES and 2 * fox_hd == LANES
    assert n_fox % 2 == 0 and T_NEW & (T_NEW - 1) == 0 and T_NEW <= SUBLANES // 2
    assert (DB * T_NEW) % LANES == 0 and S % LANES == 0
    da_scale, fox_scale = da_qk ** -0.5, fox_hd ** -0.5
    t = _tile(S, PROMPT_TILE)
    rs = DB * T_NEW
    pages = _tile(n_pages, DECODE_PAGES)
    new_pad = 2 * SUBLANES
    ffn_chunk = _tile(dff, FFN_CHUNK)

    dkt = jnp.transpose(cache_dk, (0, 1, 3, 4, 5, 2)).reshape(depth, n_phys, dq, page)
    dv4 = cache_dv.reshape(depth, n_phys, page * n_da, da_v)
    fkt = jnp.transpose(cache_fk, (0, 1, 3, 4, 2)).reshape(depth, n_phys, dfx, page)
    fvt = jnp.transpose(cache_fv, (0, 1, 3, 4, 2)).reshape(depth, n_phys, dfx, page)
    flt = jnp.pad(jnp.swapaxes(cache_flogf, 2, 3), ((0, 0), (0, 0), (0, SUBLANES - n_fox), (0, 0)))
    pt_flat = page_table.reshape(-1)

    dtiles, bias_past, bias_new = _bias_tiles(rel_bias, t=t, past=past, new_pad=new_pad, dec_seq=T_NEW)

    xp = x_prompt
    xs = x_sample.reshape(1, rs, D)
    zeros_c = jnp.zeros((B, SUBLANES, cw), F32)
    zeros_f = jnp.zeros((B, SUBLANES, 2 * dff), F32)
    eye_da = jnp.eye(n_da, dtype=BF16)
    pair_of_head = (jnp.arange(n_fox)[:, None] // 2 == jnp.arange(n_fox // 2)[None, :]).astype(BF16)
    st_p, st_s, leaves_p = [], [], None
    wo, wu, wd = w_out.astype(BF16), w_up.astype(BF16), w_down.astype(BF16)
    for l in range(depth):
        lam_init = 0.8 - 0.6 * math.exp(-0.3 * l)
        final = l == depth - 1
        wl = w_in[l]
        s0, s1 = 3 * cw, 3 * cw + dq
        wn = jnp.concatenate([wl[:, :s1], wl[:, s1 + dq:s1 + 2 * dq], wl[:, s1 + 2 * dq:s1 + 2 * dq + dfx]],
                             axis=1).astype(BF16)
        wt = jnp.concatenate([wl[:, s1:s1 + dq], wl[:, s1 + 2 * dq + dfx:s1 + 2 * dq + 3 * dfx]], axis=1).T.astype(BF16)
        wfl = _pad_rows(wl[:, s1 + 2 * dq + 3 * dfx:].T, 2 * SUBLANES).astype(BF16)
        bft = _pad_rows(b_f[l][:, None], SUBLANES)
        cwp = _pad_rows(conv_w[l], SUBLANES)
        fwp = _pad_rows(ffn_conv[l], SUBLANES)
        g1, g2, gf, gs = norm1[l][None], norm2[l][None], norm_f[None], subln[l][None]
        lam_p = diff_lambda[l]
        inproj = functools.partial(_inproj, da_scale=da_scale * LOG2E, fox_scale=fox_scale * LOG2E)

        (conv, cst, qz, v, vb, qfz, kt, kft, vft, lft, ktb, kftb, vftb, ct, ccol) = inproj(
            xp, g1, wn, wt, wfl, bft, cwp, zeros_c, tm=t, period=t, prompt=True, prev=leaves_p)
        leaves_p = (v, kt, kft, vft, lft)
        od = _diff_prompt(qz, ktb, vb, dtiles, lam_p, gs, t=t, lam_init=lam_init)
        of = _fox_prompt(qfz, kftb, vftb, ct, ccol, t=t)
        xp, fst = _outffn(xp, conv, od, of, wo, g2, wu, fwp, wd, gf, zeros_f,
                          layer=l, tm=t, period=t, prompt=True, final=final, chunk_w=ffn_chunk)
        st_p.append((cst[:, SUBLANES - 2:, :], fst[:, SUBLANES - 2:, :]))

        pc = (jnp.repeat(state_conv[l][:, 0], T_NEW, axis=0)[None], jnp.repeat(state_conv[l][:, 1], T_NEW, axis=0)[None])
        pf = (jnp.repeat(state_ffn[l][:, 0], T_NEW, axis=0)[None], jnp.repeat(state_ffn[l][:, 1], T_NEW, axis=0)[None])
        (conv_s, gcu_s, qz_s, v_s, vb_s, qfz_s, kt_s, kft_s, vft_s, lft_s) = inproj(
            xs, g1, wn, wt, wfl, bft, cwp, pc, tm=rs, period=T_NEW, prompt=False)
        k_s = kt_s[0].T.reshape(DB, T_NEW, dq)
        kf_s = kft_s[0].T.reshape(DB, T_NEW, dfx)
        vf_s = vft_s[0].T.reshape(DB, T_NEW, dfx)
        lf_s = lft_s[0, :n_fox].T.reshape(DB, T_NEW, n_fox)
        pad_new = lambda a: jnp.pad(a, ((0, 0), (0, new_pad - T_NEW), (0, 0)))
        pad_lane = lambda a: jnp.pad(a, ((0, 0), (0, LANES - T_NEW), (0, 0)))
        q5 = jnp.transpose(qz_s.reshape(DB, T_NEW, n_da, 2, LANES), (0, 2, 3, 1, 4))
        qbd = (q5[:, :, :, :, None, :] * eye_da[None, :, None, None, :, None]).reshape(DB, n_da * 2 * T_NEW, dq)
        qf4 = jnp.transpose(qfz_s.reshape(DB, T_NEW, n_fox, LANES), (0, 2, 1, 3))
        qfbd = (qf4[:, :, :, None, :] * pair_of_head[None, :, None, :, None]).reshape(DB, n_fox * T_NEW, dfx)
        lnew = jnp.pad(jnp.broadcast_to(jnp.swapaxes(lf_s, 1, 2)[:, :, None, :], (DB, n_fox, T_NEW, T_NEW)),
                       ((0, 0), (0, 0), (0, 0), (0, LANES - T_NEW))).reshape(DB, n_fox * T_NEW, LANES)
        od_s, of_s = _decode(pt_flat, l, qbd, dkt, dv4, bias_past, pad_new(k_s).astype(BF16),
                             pad_new(vb_s.reshape(DB, T_NEW, dq)), bias_new, lam_p, gs,
                             qfbd, fkt, fvt, flt, pad_lane(kf_s).astype(BF16),
                             jnp.swapaxes(pad_lane(vf_s), 1, 2).astype(BF16), lnew,
                             pages=pages, n_pages=n_pages, lam_init=lam_init, dec_seq=T_NEW)
        od_s = jnp.transpose(od_s.reshape(DB, n_da, 2, T_NEW, da_v)[:, :, 0], (0, 2, 1, 3)).reshape(1, rs, dq)
        of5 = of_s.reshape(DB, n_fox, T_NEW, n_fox, fox_hd)
        of_s = jnp.transpose(jnp.einsum('bhthd->bhtd', of5), (0, 2, 1, 3)).reshape(1, rs, dfx)
        xs, up_s = _outffn(xs, conv_s, od_s.astype(BF16), of_s.astype(BF16), wo, g2, wu, fwp, wd, gf, pf,
                           layer=l, tm=rs, period=T_NEW, prompt=False, final=final, chunk_w=ffn_chunk)
        st_s.append((
            k_s.reshape(DB, T_NEW, n_da, 2, da_qk), v_s.reshape(DB, T_NEW, n_da, da_v),
            kf_s.reshape(DB, T_NEW, n_fox, fox_hd), vf_s.reshape(DB, T_NEW, n_fox, fox_hd), lf_s,
            gcu_s.reshape(DB, T_NEW, cw)[:, T_NEW - 2:], up_s.reshape(DB, T_NEW, 2 * dff)[:, T_NEW - 2:]))

    v_all, kt_all, kft_all, vft_all, lft_all = leaves_p
    prompt_leaves = [
        jnp.transpose(kt_all.reshape(depth, B, n_da, 2, da_qk, S), (0, 1, 5, 2, 3, 4)),
        v_all.reshape(depth, B, S, n_da, da_v),
        jnp.transpose(kft_all.reshape(depth, B, n_fox, fox_hd, S), (0, 1, 4, 2, 3)),
        jnp.transpose(vft_all.reshape(depth, B, n_fox, fox_hd, S), (0, 1, 4, 2, 3)),
        jnp.swapaxes(lft_all[:, :, :n_fox, :], 2, 3),
        jnp.stack([s[0] for s in st_p]), jnp.stack([s[1] for s in st_p])]
    outs = [xp, xs.reshape(DB, T_NEW, D)]
    for j in range(7):
        outs.append(prompt_leaves[j])
        outs.append(jnp.stack([s[j] for s in st_s]))
    return tuple(outs)
```
